```python
import math
import jax, jax.numpy as jnp
from jax import lax
import numpy as np


D_MODEL = 2048
BATCH = 8
SEQ = 2048
DEPTH = 2
DEC_BATCH = 128
DEC_SEQ = 4
PAST_LEN = 2048
PAGE_SIZE = 128

N_A_LAYERS = DEPTH // 2
N_B_LAYERS = DEPTH - N_A_LAYERS
PLE_DIM = 256
SSM_EXPAND = 2
D_INNER = SSM_EXPAND * D_MODEL
SSM_HEAD_DIM = 64
SSM_HEADS = D_INNER // SSM_HEAD_DIM
SSM_GROUPS = 8
SSM_STATE = 128
CONV_WIDTH = 4
CONV_DIM = D_INNER + 2 * SSM_GROUPS * SSM_STATE
IN_PROJ_DIM = D_INNER + CONV_DIM + SSM_HEADS
SSD_CHUNK = 128
ATT_HEAD_DIM = 128
ATT_HEADS = D_MODEL // ATT_HEAD_DIM
D_ATT = ATT_HEADS * ATT_HEAD_DIM
KV_PROJ_DIM = 2 * D_ATT + ATT_HEADS
Q_BLOCK = 128
D_FF = -(-8 * D_MODEL // (3 * 256)) * 256
ALPHA = (2 * DEPTH) ** 0.25
BETA = (8 * DEPTH) ** -0.25
LN_EPS = 1e-5
RMS_EPS = 1e-5
NEG_INF = -1e30

kernel_name = 'yoco_ssd_fox_decoder_step'

F32 = jnp.float32


def layer_norm(x, g, b):
    xf = x.astype(F32)
    mu = jnp.mean(xf, -1, keepdims=True)
    var = jnp.mean(jnp.square(xf - mu), -1, keepdims=True)
    return ((xf - mu) * lax.rsqrt(var + LN_EPS) * g + b).astype(x.dtype)


def swiglu(x, w_gate, w_up, w_down):
    return (jax.nn.silu(x @ w_gate) * (x @ w_up)) @ w_down


def ssd_scan(x, dt, a, bm, cm, h0):
    b, L, H, P = x.shape
    G, N = bm.shape[2], bm.shape[3]
    E = H // G
    cl = SSD_CHUNK if L % SSD_CHUNK == 0 else L
    nc = L // cl
    x = x.reshape(b, nc, cl, G, E, P)
    dt = dt.reshape(b, nc, cl, G, E)
    bm = bm.reshape(b, nc, cl, G, N)
    cm = cm.reshape(b, nc, cl, G, N)
    a_cum = jnp.cumsum(dt * a.reshape(G, E), axis=2)
    xdt = x * dt[..., None]
    idx = jnp.arange(cl)
    causal = (idx[:, None] >= idx[None, :])[None, None, :, :, None, None]
    seg = a_cum[:, :, :, None] - a_cum[:, :, None, :]
    decay = jnp.exp(jnp.where(causal, seg, -jnp.inf))
    cb = jnp.einsum('bclgn,bcsgn->bclsg', cm, bm)
    y_diag = jnp.einsum('bclsge,bcsgep->bclgep', cb[..., None] * decay, xdt)
    decay_end = jnp.exp(a_cum[:, :, -1:] - a_cum)
    states = jnp.einsum('bclgn,bclgep->bcgepn', bm, xdt * decay_end[..., None])
    chunk_decay = jnp.exp(a_cum[:, :, -1])

    def step(h, inp):
        s, d = inp
        return d[..., None, None] * h + s, h

    h_final, h_prev = lax.scan(step, h0.reshape(b, G, E, P, N),
                               (states.transpose(1, 0, 2, 3, 4, 5), chunk_decay.transpose(1, 0, 2, 3)))
    h_prev = h_prev.transpose(1, 0, 2, 3, 4, 5)
    y_off = jnp.einsum('bclgn,bcgepn->bclgep', cm, h_prev) * jnp.exp(a_cum)[..., None]
    y = (y_diag + y_off).reshape(b, L, H, P)
    return y, h_final.reshape(b, H, P, N)


def mamba_mixer(x, conv_state, ssm_state, w_in, conv_w, conv_b, dt_bias, a_log, d_skip, norm_w, w_out):
    b, L, _ = x.shape
    zxbcdt = x @ w_in
    z = zxbcdt[..., :D_INNER]
    xbc = zxbcdt[..., D_INNER:D_INNER + CONV_DIM]
    dt_raw = zxbcdt[..., D_INNER + CONV_DIM:]
    xpad = jnp.concatenate([conv_state.astype(xbc.dtype), xbc], axis=1)
    conv = conv_b + sum(xpad[:, k:k + L] * conv_w[k] for k in range(CONV_WIDTH))
    new_conv = xpad[:, L:]
    xbc = jax.nn.silu(conv)
    gn = SSM_GROUPS * SSM_STATE
    xs = xbc[..., :D_INNER].reshape(b, L, SSM_HEADS, SSM_HEAD_DIM).astype(F32)
    bm = xbc[..., D_INNER:D_INNER + gn].reshape(b, L, SSM_GROUPS, SSM_STATE).astype(F32)
    cm = xbc[..., D_INNER + gn:].reshape(b, L, SSM_GROUPS, SSM_STATE).astype(F32)
    dt = jax.nn.softplus(dt_raw.astype(F32) + dt_bias.astype(F32))
    a = -jnp.exp(a_log.astype(F32))
    y, h = ssd_scan(xs, dt, a, bm, cm, ssm_state.astype(F32))
    y = y + d_skip.astype(F32)[:, None] * xs
    gz = y.reshape(b, L, D_INNER) * jax.nn.silu(z.astype(F32))
    gz = gz.reshape(b, L, SSM_GROUPS, D_INNER // SSM_GROUPS)
    gz = gz * lax.rsqrt(jnp.mean(jnp.square(gz), -1, keepdims=True) + RMS_EPS)
    gz = gz.reshape(b, L, D_INNER) * norm_w
    return gz.astype(x.dtype) @ w_out, new_conv, h


def fox_prompt(q, k, v, logf):
    b, L, H, Dh = q.shape
    qf = q.astype(F32) * (Dh ** -0.5)
    kf = k.astype(F32)
    vf = v.astype(F32)
    F = jnp.cumsum(logf.astype(F32), axis=1).transpose(0, 2, 1)
    kpos = jnp.arange(L)

    def block(start):
        qb = lax.dynamic_slice_in_dim(qf, start, Q_BLOCK, axis=1)
        fq = lax.dynamic_slice_in_dim(F, start, Q_BLOCK, axis=2)
        s = jnp.einsum('bqhd,bkhd->bhqk', qb, kf) + (fq[..., :, None] - F[..., None, :])
        mask = (start + jnp.arange(Q_BLOCK))[:, None] >= kpos[None, :]
        p = jax.nn.softmax(jnp.where(mask, s, NEG_INF), axis=-1)
        return jnp.einsum('bhqk,bkhd->bqhd', p, vf)

    out = lax.map(block, jnp.arange(0, L, Q_BLOCK))
    return out.transpose(1, 0, 2, 3, 4).reshape(b, L, H, Dh).astype(q.dtype)


def fox_sample(q, k_new, v_new, logf_new, cache_k, cache_v, cache_logf, page_table):
    T, H, Dh = q.shape[1], q.shape[2], q.shape[3]
    scale = Dh ** -0.5

    def one(args):
        qs, kn, vn, fn, pages = args
        kp = cache_k[pages].reshape(-1, H, Dh)
        vp = cache_v[pages].reshape(-1, H, Dh)
        fp = cache_logf[pages].reshape(-1, H)
        past = kp.shape[0]
        k_all = jnp.concatenate([kp.astype(F32), kn.astype(F32)], axis=0)
        v_all = jnp.concatenate([vp.astype(F32), vn.astype(F32)], axis=0)
        F = jnp.cumsum(jnp.concatenate([fp.astype(F32), fn.astype(F32)], axis=0), axis=0).T
        s = jnp.einsum('qhd,khd->hqk', qs.astype(F32) * scale, k_all) + (F[:, past:, None] - F[:, None, :])
        mask = (past + jnp.arange(T))[:, None] >= jnp.arange(past + T)[None, :]
        p = jax.nn.softmax(jnp.where(mask, s, NEG_INF), axis=-1)
        return jnp.einsum('hqk,khd->qhd', p, v_all)

    return lax.map(one, (q, k_new, v_new, logf_new, page_table)).astype(q.dtype)


def setup_inputs(seed: int = 0) -> dict:
    key = jax.random.key(seed)
    keys = iter(jax.random.split(key, 64))

    def nrm(shape, scale=1.0):
        return scale * jax.random.normal(next(keys), shape, F32)

    n_pages = PAST_LEN // PAGE_SIZE
    n_pool = (5 * DEC_BATCH * n_pages) // 4
    perm = jax.random.permutation(next(keys), n_pool)
    page_table = perm[:DEC_BATCH * n_pages].reshape(DEC_BATCH, n_pages).astype(jnp.int32)
    dt0 = jnp.exp(jax.random.uniform(next(keys), (N_A_LAYERS, SSM_HEADS), F32, math.log(1e-3), math.log(1e-1)))
    a_dt_bias = dt0 + jnp.log(-jnp.expm1(-dt0))
    a_log = jnp.log(jax.random.uniform(next(keys), (N_A_LAYERS, SSM_HEADS), F32, 1.0, 16.0))
    kv_w = jnp.concatenate([nrm((D_MODEL, 2 * D_ATT), D_MODEL ** -0.5),
                            nrm((D_MODEL, ATT_HEADS), 0.5 * D_MODEL ** -0.5)], axis=1)
    return {
        'x_prompt': nrm((BATCH, SEQ, D_MODEL)),
        'x_sample': nrm((DEC_BATCH, DEC_SEQ, D_MODEL)),
        'p_prompt': nrm((DEPTH, BATCH, SEQ, PLE_DIM)),
        'p_sample': nrm((DEPTH, DEC_BATCH, DEC_SEQ, PLE_DIM)),
        'state_conv': nrm((N_A_LAYERS, DEC_BATCH, CONV_WIDTH - 1, CONV_DIM)),
        'state_ssm': nrm((N_A_LAYERS, DEC_BATCH, SSM_HEADS, SSM_HEAD_DIM, SSM_STATE), 0.1),
        'cache_k': nrm((n_pool, PAGE_SIZE, ATT_HEADS, ATT_HEAD_DIM)),
        'cache_v': nrm((n_pool, PAGE_SIZE, ATT_HEADS, ATT_HEAD_DIM)),
        'cache_logf': jax.nn.log_sigmoid(2.0 + nrm((n_pool, PAGE_SIZE, ATT_HEADS), 0.5)),
        'page_table': page_table,
        'a_w_in': nrm((N_A_LAYERS, D_MODEL, IN_PROJ_DIM), D_MODEL ** -0.5),
        'a_conv_w': nrm((N_A_LAYERS, CONV_WIDTH, CONV_DIM), CONV_WIDTH ** -0.5),
        'a_conv_b': nrm((N_A_LAYERS, CONV_DIM), 0.01),
        'a_dt_bias': a_dt_bias,
        'a_log': a_log,
        'a_d': 1.0 + nrm((N_A_LAYERS, SSM_HEADS), 0.1),
        'a_norm_w': 1.0 + nrm((N_A_LAYERS, D_INNER), 0.05),
        'a_w_out': nrm((N_A_LAYERS, D_INNER, D_MODEL), BETA * D_INNER ** -0.5),
        'kv_w': kv_w,
        'kv_b_f': 2.0 + nrm((ATT_HEADS,), 0.1),
        'b_w_q': nrm((N_B_LAYERS, D_MODEL, D_ATT), D_MODEL ** -0.5),
        'b_w_o': nrm((N_B_LAYERS, D_ATT, D_MODEL), BETA * D_ATT ** -0.5),
        'ln1_g': 1.0 + nrm((DEPTH, D_MODEL), 0.05),
        'ln1_b': nrm((DEPTH, D_MODEL), 0.01),
        'ffn_w_gate': nrm((DEPTH, D_MODEL, D_FF), D_MODEL ** -0.5),
        'ffn_w_up': nrm((DEPTH, D_MODEL, D_FF), D_MODEL ** -0.5),
        'ffn_w_down': nrm((DEPTH, D_FF, D_MODEL), BETA * D_FF ** -0.5),
        'ln2_g': 1.0 + nrm((DEPTH, D_MODEL), 0.05),
        'ln2_b': nrm((DEPTH, D_MODEL), 0.01),
        'ple_w_gate': nrm((DEPTH, D_MODEL, D_MODEL), D_MODEL ** -0.5),
        'ple_w_proj': nrm((DEPTH, PLE_DIM, D_MODEL), PLE_DIM ** -0.5),
    }


def reference(x_prompt, x_sample, p_prompt, p_sample, state_conv, state_ssm, cache_k, cache_v, cache_logf,
              page_table, a_w_in, a_conv_w, a_conv_b, a_dt_bias, a_log, a_d, a_norm_w, a_w_out, kv_w, kv_b_f,
              b_w_q, b_w_o, ln1_g, ln1_b, ffn_w_gate, ffn_w_up, ffn_w_down, ln2_g, ln2_b, ple_w_gate, ple_w_proj):

    def run(x, p, conv0, ssm0, attend):
        convs, ssms = [], []
        shared = None
        for i in range(DEPTH):
            bsz, L = x.shape[0], x.shape[1]
            if i < N_A_LAYERS:
                a = i
                h, cs, ss = mamba_mixer(x, conv0[a], ssm0[a], a_w_in[a], a_conv_w[a], a_conv_b[a], a_dt_bias[a],
                                        a_log[a], a_d[a], a_norm_w[a], a_w_out[a])
                convs.append(cs)
                ssms.append(ss)
            else:
                bi = i - N_A_LAYERS
                q = (x @ b_w_q[bi]).reshape(bsz, L, ATT_HEADS, ATT_HEAD_DIM)
                o = attend(q, shared[0], shared[1], shared[2])
                h = o.reshape(bsz, L, D_ATT) @ b_w_o[bi]
            x = layer_norm(ALPHA * x + h, ln1_g[i], ln1_b[i])
            x = layer_norm(ALPHA * x + swiglu(x, ffn_w_gate[i], ffn_w_up[i], ffn_w_down[i]), ln2_g[i], ln2_b[i])
            x = x + jax.nn.sigmoid(x @ ple_w_gate[i]) * (p[i] @ ple_w_proj[i])
            if i == N_A_LAYERS - 1:
                kvf = x @ kv_w
                k = kvf[..., :D_ATT].reshape(bsz, L, ATT_HEADS, ATT_HEAD_DIM)
                v = kvf[..., D_ATT:2 * D_ATT].reshape(bsz, L, ATT_HEADS, ATT_HEAD_DIM)
                logf = jax.nn.log_sigmoid(kvf[..., 2 * D_ATT:].astype(F32) + kv_b_f.astype(F32))
                shared = (k, v, logf)
        return x, jnp.stack(convs), jnp.stack(ssms), shared

    bp = x_prompt.shape[0]
    conv0_p = jnp.zeros((N_A_LAYERS, bp, CONV_WIDTH - 1, CONV_DIM), x_prompt.dtype)
    ssm0_p = jnp.zeros((N_A_LAYERS, bp, SSM_HEADS, SSM_HEAD_DIM, SSM_STATE), F32)
    y_prompt, conv_p, ssm_p, kv_p = run(x_prompt, p_prompt, conv0_p, ssm0_p, fox_prompt)

    def attend_sample(q, k, v, f):
        return fox_sample(q, k, v, f, cache_k, cache_v, cache_logf, page_table)

    y_sample, conv_s, ssm_s, kv_s = run(x_sample, p_sample, state_conv, state_ssm, attend_sample)
    return (y_prompt, y_sample, conv_p, ssm_p, kv_p[0], kv_p[1], kv_p[2], conv_s, ssm_s, kv_s[0], kv_s[1], kv_s[2])
```

```python
import functools

import jax
import jax.numpy as jnp
from jax import lax
from jax.experimental import pallas as pl
from jax.experimental.pallas import tpu as pltpu

F32 = jnp.float32
BF16 = jnp.bfloat16

DEPTH = 2
SSM_HEAD_DIM = 64
SSM_GROUPS = 8
SSM_STATE = 128
CONV_WIDTH = 4
SSD_CHUNK = 128
ATT_HEAD_DIM = 128
ALPHA = (2 * DEPTH) ** 0.25
LN_EPS = 1e-5
RMS_EPS = 1e-5
NEG_INF = -1e30

LANES = 128
SUBLANES = 8
V7X_VMEM_BYTES = 64 * 1024 * 1024
VMEM_LIMIT = V7X_VMEM_BYTES * 7 // 8


def _cparams(*sem):
    return pltpu.CompilerParams(dimension_semantics=sem, vmem_limit_bytes=VMEM_LIMIT)


def _dot(a, b):
    return jnp.dot(a, b, preferred_element_type=F32)


def _dot_nt(a, b):
    return lax.dot_general(a, b, (((1,), (1,)), ((), ())), preferred_element_type=F32)


def _split_bf16(x, parts):
    out = []
    r = x
    for i in range(parts):
        p = r.astype(BF16)
        out.append(p)
        if i + 1 < parts:
            r = r - p.astype(F32)
    return out


def _sel_dot(sel, x, parts):
    acc = None
    for p in _split_bf16(x, parts):
        t = _dot(sel, p)
        acc = t if acc is None else acc + t
    return acc


def _dot_sel(x, sel, parts):
    acc = None
    for p in _split_bf16(x, parts):
        t = _dot(p, sel)
        acc = t if acc is None else acc + t
    return acc


def _softplus(x):
    return jnp.maximum(x, 0.0) + jnp.log1p(jnp.exp(-jnp.abs(x)))


def _log_sigmoid(x):
    return jnp.minimum(x, 0.0) - jnp.log1p(jnp.exp(-jnp.abs(x)))


def _layer_norm(v, g, b):
    mu = jnp.mean(v, axis=-1, keepdims=True)
    d = v - mu
    var = jnp.mean(d * d, axis=-1, keepdims=True)
    return d * lax.rsqrt(var + LN_EPS) * g + b


def _tile(n, cap):
    t = min(n, cap)
    assert n % t == 0, (n, cap)
    return t


def _proj_kernel(*refs, mode, scale):
    if mode == "logsig":
        x_ref, w_ref, b_ref, o_ref = refs
    elif mode == "dual":
        x_ref, w_ref, o_ref, ob_ref = refs
    else:
        x_ref, w_ref, o_ref = refs
    acc = _dot(x_ref[...], w_ref[...])
    if mode == "logsig":
        o_ref[...] = _log_sigmoid(acc + b_ref[...])
    elif mode == "dual":
        o_ref[...] = acc
        ob_ref[...] = acc.astype(BF16)
    elif mode == "scale":
        o_ref[...] = (acc * scale).astype(o_ref.dtype)
    else:
        o_ref[...] = acc.astype(o_ref.dtype)


def _proj(xb, w, *, mode="plain", out_dtype=F32, scale=1.0, bias=None, name):
    m, k = xb.shape
    n = w.shape[1]
    tm = _tile(m, 512)
    tn = _tile(n, 1024)
    in_specs = [pl.BlockSpec((tm, k), lambda i, j: (i, 0)), pl.BlockSpec((k, tn), lambda i, j: (0, j))]
    args = [xb, w]
    o_spec = pl.BlockSpec((tm, tn), lambda i, j: (i, j))
    if mode == "logsig":
        in_specs.append(pl.BlockSpec((1, tn), lambda i, j: (0, j)))
        args.append(bias)
    if mode == "dual":
        out_shape = (jax.ShapeDtypeStruct((m, n), F32), jax.ShapeDtypeStruct((m, n), BF16))
        out_specs = (o_spec, o_spec)
    else:
        out_shape = jax.ShapeDtypeStruct((m, n), out_dtype)
        out_specs = o_spec
    return pl.pallas_call(
        functools.partial(_proj_kernel, mode=mode, scale=scale),
        out_shape=out_shape,
        grid=(m // tm, n // tn),
        in_specs=in_specs,
        out_specs=out_specs,
        compiler_params=_cparams("parallel", "arbitrary"),
        name=name,
    )(*args)


CONV_PAD = SUBLANES
CONV_ROWS = 256


def _conv_kernel(x_ref, st_ref, w_ref, b_ref, act_ref, nc_ref, pad_ref, *, seq):
    lo = CONV_PAD - (CONV_WIDTH - 1)
    pad_ref[lo:CONV_PAD, :] = st_ref[0]
    pad_ref[CONV_PAD:CONV_PAD + seq, :] = x_ref[0]
    w = w_ref[...]
    b = b_ref[...]
    rc = min(seq, CONV_ROWS)
    for r0 in range(0, seq, rc):
        acc = b + w[0:1, :] * pad_ref[lo + r0:lo + r0 + rc, :]
        for k in range(1, CONV_WIDTH):
            acc = acc + w[k:k + 1, :] * pad_ref[lo + k + r0:lo + k + r0 + rc, :]
        act_ref[0, r0:r0 + rc, :] = acc * jax.nn.sigmoid(acc)
    nc_ref[0] = pad_ref[lo + seq:CONV_PAD + seq, :]


def _conv_silu(zx, conv_state, conv_w, conv_b, *, d_inner, name):
    bsz, seq, _ = zx.shape
    conv_dim = conv_w.shape[1]
    tc = 512 if seq >= SSD_CHUNK else 2048
    assert conv_dim % tc == 0 and d_inner % tc == 0
    off = d_inner // tc
    return pl.pallas_call(
        functools.partial(_conv_kernel, seq=seq),
        out_shape=(jax.ShapeDtypeStruct((bsz, seq, conv_dim), F32),
                   jax.ShapeDtypeStruct((bsz, CONV_WIDTH - 1, conv_dim), F32)),
        grid=(bsz, conv_dim // tc),
        in_specs=[
            pl.BlockSpec((1, seq, tc), lambda b, j: (b, 0, off + j)),
            pl.BlockSpec((1, CONV_WIDTH - 1, tc), lambda b, j: (b, 0, j)),
            pl.BlockSpec((CONV_WIDTH, tc), lambda b, j: (0, j)),
            pl.BlockSpec((1, tc), lambda b, j: (0, j)),
        ],
        out_specs=(
            pl.BlockSpec((1, seq, tc), lambda b, j: (b, 0, j)),
            pl.BlockSpec((1, CONV_WIDTH - 1, tc), lambda b, j: (b, 0, j)),
        ),
        scratch_shapes=[pltpu.VMEM((CONV_PAD + seq, tc), F32)],
        compiler_params=_cparams("parallel", "parallel"),
        name=name,
    )(zx, conv_state, conv_w, conv_b.reshape(1, conv_dim))


def _ssd_kernel(*refs, rows, has_h0, d_inner, n_state):
    cl = SSD_CHUNK
    groups = SSM_GROUPS
    gw = d_inner // groups
    pairs = gw // LANES
    heads_per_group = gw // SSM_HEAD_DIM
    padded = rows < cl
    it = iter(refs)
    xs_ref, bm_ref, cm_ref, z_ref, dt_ref = next(it), next(it), next(it), next(it), next(it)
    dtb_ref, alog_ref, dsk_ref, nw_ref, tri_ref, exp_ref = (next(it), next(it), next(it), next(it), next(it), next(it))
    h0_ref = next(it) if has_h0 else None
    g_ref, hout_ref = next(it), next(it)
    ht_ref = next(it)
    if padded:
        xs_p, bm_p, cm_p, z_p, dt_p = next(it), next(it), next(it), next(it), next(it)

    c = pl.program_id(1)

    @pl.when(c == 0)
    def _init():
        if has_h0:
            for i in range(d_inner // LANES):
                ht_ref[:, i * LANES:(i + 1) * LANES] = h0_ref[0, i * LANES:(i + 1) * LANES, :].T
        else:
            ht_ref[...] = jnp.zeros_like(ht_ref)

    if padded:
        for src, dst in ((xs_ref, xs_p), (bm_ref, bm_p), (cm_ref, cm_p), (z_ref, z_p), (dt_ref, dt_p)):
            dst[...] = jnp.zeros_like(dst)
            dst[0:rows, :] = src[0]
        xs_v, bm_v, cm_v, z_v, dt_v = xs_p, bm_p, cm_p, z_p, dt_p
        rd = lambda ref, sl: ref[:, sl]
    else:
        xs_v, bm_v, cm_v, z_v, dt_v = xs_ref, bm_ref, cm_ref, z_ref, dt_ref
        rd = lambda ref, sl: ref[0, :, sl]

    full = slice(None)
    dt = _softplus(rd(dt_v, full) + dtb_ref[...])
    if padded:
        row = lax.broadcasted_iota(jnp.int32, dt.shape, 0)
        dt = jnp.where(row < rows, dt, 0.0)
    a = -jnp.exp(alog_ref[...])
    tri = tri_ref[...]
    a_cum = _sel_dot(tri, dt * a, 3)
    a_cum_t = a_cum.T
    a_last = a_cum[cl - 1:cl, :]
    ea = jnp.exp(a_cum)
    de = jnp.exp(a_last - a_cum)

    li = lax.broadcasted_iota(jnp.int32, (cl, cl), 0)
    si = lax.broadcasted_iota(jnp.int32, (cl, cl), 1)
    causal = li >= si
    lane = lax.broadcasted_iota(jnp.int32, (cl, LANES), 1)
    first_head = lane < SSM_HEAD_DIM

    for g in range(groups):
        gs = slice(g * gw, (g + 1) * gw)
        ns = slice(g * n_state, (g + 1) * n_state)
        expand = exp_ref[:, gs]
        xs_g = rd(xs_v, gs)
        dt_e = _dot_sel(dt, expand, 2)
        ea_e = _dot_sel(ea, expand, 2)
        de_e = _dot_sel(de, expand, 2)
        xdt = xs_g * dt_e
        bm_t = rd(bm_v, ns).T.astype(BF16)
        cm_b = rd(cm_v, ns).astype(BF16)
        cb = _dot(cm_b, bm_t)
        ht_g = ht_ref[:, gs]
        y_off = _dot(cm_b, ht_g.astype(BF16)) * ea_e
        y_parts = []
        for j in range(pairs):
            sc = []
            for e in range(2):
                h = g * heads_per_group + 2 * j + e
                seg = a_cum[:, h:h + 1] - a_cum_t[h:h + 1, :]
                dec = jnp.exp(jnp.where(causal, seg, -jnp.inf))
                sc.append((cb * dec).astype(BF16))
            lhs = jnp.concatenate(sc, axis=1)
            xp = xdt[:, j * LANES:(j + 1) * LANES]
            rhs = jnp.concatenate([jnp.where(first_head, xp, 0.0), jnp.where(first_head, 0.0, xp)],
                                  axis=0).astype(BF16)
            y_parts.append(_dot(lhs, rhs))
        y_g = jnp.concatenate(y_parts, axis=1) + y_off + dsk_ref[:, gs] * xs_g
        ht_ref[:, gs] = ea_e[cl - 1:cl, :] * ht_g + _dot(bm_t, (xdt * de_e).astype(BF16))
        z_g = rd(z_v, gs)
        gz = y_g * (z_g * jax.nn.sigmoid(z_g))
        ms = jnp.mean(gz * gz, axis=-1, keepdims=True)
        out_g = gz * lax.rsqrt(ms + RMS_EPS) * nw_ref[:, gs]
        g_ref[0, :, gs] = out_g[0:rows, :].astype(BF16)

    @pl.when(c == pl.num_programs(1) - 1)
    def _fin():
        for i in range(d_inner // LANES):
            hout_ref[0, i * LANES:(i + 1) * LANES, :] = ht_ref[:, i * LANES:(i + 1) * LANES].T


def _ssd(zx, act, dt_raw, ssm0, dt_bias, a_log, d_skip, norm_w, *, d_inner, name):
    bsz, seq, _ = act.shape
    n_state = SSM_STATE
    heads = d_inner // SSM_HEAD_DIM
    gn = SSM_GROUPS * n_state
    assert heads <= LANES and d_inner % gn == 0 and gn % LANES == 0
    cl = SSD_CHUNK
    rows = cl if seq % cl == 0 else seq
    assert rows <= cl
    nc = seq // rows
    has_h0 = ssm0 is not None

    def pad_heads(v):
        return jnp.pad(v.astype(F32), (0, LANES - heads)).reshape(1, LANES)

    tri = (jnp.arange(cl)[:, None] >= jnp.arange(cl)[None, :]).astype(BF16)
    expand = (jnp.arange(LANES)[:, None] == (jnp.arange(d_inner) // SSM_HEAD_DIM)[None, :]).astype(BF16)
    dsk = jnp.repeat(d_skip.astype(F32), SSM_HEAD_DIM).reshape(1, d_inner)
    nw = norm_w.astype(F32).reshape(1, d_inner)
    bc_off = d_inner // gn

    const = lambda shape: pl.BlockSpec(shape, lambda b, c: (0,) * len(shape))
    in_specs = [
        pl.BlockSpec((1, rows, d_inner), lambda b, c: (b, c, 0)),
        pl.BlockSpec((1, rows, gn), lambda b, c: (b, c, bc_off)),
        pl.BlockSpec((1, rows, gn), lambda b, c: (b, c, bc_off + 1)),
        pl.BlockSpec((1, rows, d_inner), lambda b, c: (b, c, 0)),
        pl.BlockSpec((1, rows, LANES), lambda b, c: (b, c, 0)),
        const((1, LANES)), const((1, LANES)), const((1, d_inner)), const((1, d_inner)),
        const((cl, cl)), const((LANES, d_inner)),
    ]
    args = [act, act, act, zx, dt_raw, pad_heads(dt_bias), pad_heads(a_log), dsk, nw, tri, expand]
    if has_h0:
        in_specs.append(pl.BlockSpec((1, d_inner, n_state), lambda b, c: (b, 0, 0)))
        args.append(ssm0.reshape(bsz, d_inner, n_state))
    scratch = [pltpu.VMEM((n_state, d_inner), F32)]
    if rows < cl:
        scratch += [pltpu.VMEM((cl, d_inner), F32), pltpu.VMEM((cl, gn), F32), pltpu.VMEM((cl, gn), F32),
                    pltpu.VMEM((cl, d_inner), F32), pltpu.VMEM((cl, LANES), F32)]
    g, hout = pl.pallas_call(
        functools.partial(_ssd_kernel, rows=rows, has_h0=has_h0, d_inner=d_inner, n_state=n_state),
        out_shape=(jax.ShapeDtypeStruct((bsz, seq, d_inner), BF16),
                   jax.ShapeDtypeStruct((bsz, d_inner, n_state), F32)),
        grid=(bsz, nc),
        in_specs=in_specs,
        out_specs=(pl.BlockSpec((1, rows, d_inner), lambda b, c: (b, c, 0)),
                   pl.BlockSpec((1, d_inner, n_state), lambda b, c: (b, 0, 0))),
        scratch_shapes=scratch,
        compiler_params=_cparams("parallel", "arbitrary"),
        name=name,
    )(*args)
    return g, hout.reshape(bsz, heads, SSM_HEAD_DIM, n_state)


def _mm_ln_kernel(x_ref, w_ref, res_ref, g_ref, b_ref, o_ref, ob_ref, acc_ref):
    k = pl.program_id(1)

    @pl.when(k == 0)
    def _():
        acc_ref[...] = jnp.zeros_like(acc_ref)

    acc_ref[...] += _dot(x_ref[...], w_ref[...])

    @pl.when(k == pl.num_programs(1) - 1)
    def _():
        y = _layer_norm(ALPHA * res_ref[...] + acc_ref[...], g_ref[...], b_ref[...])
        o_ref[...] = y
        ob_ref[...] = y.astype(BF16)


def _mm_ln(xb, w, res, gamma, beta, *, name):
    m, kdim = xb.shape
    n = w.shape[1]
    tm = _tile(m, 512)
    tk = _tile(kdim, 1024)
    row = lambda shape: pl.BlockSpec(shape, lambda i, k: (i, 0))
    return pl.pallas_call(
        _mm_ln_kernel,
        out_shape=(jax.ShapeDtypeStruct((m, n), F32), jax.ShapeDtypeStruct((m, n), BF16)),
        grid=(m // tm, kdim // tk),
        in_specs=[
            pl.BlockSpec((tm, tk), lambda i, k: (i, k)),
            pl.BlockSpec((tk, n), lambda i, k: (k, 0)),
            row((tm, n)),
            pl.BlockSpec((1, n), lambda i, k: (0, 0)),
            pl.BlockSpec((1, n), lambda i, k: (0, 0)),
        ],
        out_specs=(row((tm, n)), row((tm, n))),
        scratch_shapes=[pltpu.VMEM((tm, n), F32)],
        compiler_params=_cparams("parallel", "arbitrary"),
        name=name,
    )(xb, w, res, gamma.reshape(1, n), beta.reshape(1, n))


def _ffn_kernel(x_ref, xb_ref, wg_ref, wu_ref, wd_ref, g_ref, b_ref, o_ref, ob_ref, acc_ref):
    f = pl.program_id(1)

    @pl.when(f == 0)
    def _():
        acc_ref[...] = jnp.zeros_like(acc_ref)

    xb = xb_ref[...]
    a = _dot(xb, wg_ref[...])
    u = _dot(xb, wu_ref[...])
    h = (a * jax.nn.sigmoid(a) * u).astype(BF16)
    acc_ref[...] += _dot(h, wd_ref[...])

    @pl.when(f == pl.num_programs(1) - 1)
    def _():
        y = _layer_norm(ALPHA * x_ref[...] + acc_ref[...], g_ref[...], b_ref[...])
        o_ref[...] = y
        ob_ref[...] = y.astype(BF16)


def _ffn(x, xb, wg, wu, wd, gamma, beta, *, name):
    m, d = x.shape
    dff = wg.shape[1]
    tm = _tile(m, 512)
    tf = _tile(dff, 512)
    row = lambda shape: pl.BlockSpec(shape, lambda i, f: (i, 0))
    vec = pl.BlockSpec((1, d), lambda i, f: (0, 0))
    return pl.pallas_call(
        _ffn_kernel,
        out_shape=(jax.ShapeDtypeStruct((m, d), F32), jax.ShapeDtypeStruct((m, d), BF16)),
        grid=(m // tm, dff // tf),
        in_specs=[
            row((tm, d)), row((tm, d)),
            pl.BlockSpec((d, tf), lambda i, f: (0, f)),
            pl.BlockSpec((d, tf), lambda i, f: (0, f)),
            pl.BlockSpec((tf, d), lambda i, f: (f, 0)),
            vec, vec,
        ],
        out_specs=(row((tm, d)), row((tm, d))),
        scratch_shapes=[pltpu.VMEM((tm, d), F32)],
        compiler_params=_cparams("parallel", "arbitrary"),
        name=name,
    )(x, xb, wg, wu, wd, gamma.reshape(1, d), beta.reshape(1, d))


def _ple_kernel(xs_ref, xb_ref, p_ref, wg_ref, wp_ref, o_ref, ob_ref):
    gate = _dot(xb_ref[...], wg_ref[...])
    proj = _dot(p_ref[...].astype(BF16), wp_ref[...])
    y = xs_ref[...] + jax.nn.sigmoid(gate) * proj
    o_ref[...] = y
    ob_ref[...] = y.astype(BF16)


def _ple(x, xb, p, w_gate, w_proj, *, name):
    m, d = x.shape
    pd = p.shape[1]
    tm = _tile(m, 512)
    tn = _tile(d, 1024)
    tile = pl.BlockSpec((tm, tn), lambda i, j: (i, j))
    return pl.pallas_call(
        _ple_kernel,
        out_shape=(jax.ShapeDtypeStruct((m, d), F32), jax.ShapeDtypeStruct((m, d), BF16)),
        grid=(m // tm, d // tn),
        in_specs=[
            tile,
            pl.BlockSpec((tm, d), lambda i, j: (i, 0)),
            pl.BlockSpec((tm, pd), lambda i, j: (i, 0)),
            pl.BlockSpec((d, tn), lambda i, j: (0, j)),
            pl.BlockSpec((pd, tn), lambda i, j: (0, j)),
        ],
        out_specs=(tile, tile),
        compiler_params=_cparams("parallel", "arbitrary"),
        name=name,
    )(x, xb, p, w_gate, w_proj)


def _cumsum_kernel(x_ref, tri_ref, f_ref, ft_ref, *, seq, heads):
    cl = SSD_CHUNK
    tri = tri_ref[...]
    carry = jnp.zeros((1, LANES), F32)
    for c in range(seq // cl):
        rs = slice(c * cl, (c + 1) * cl)
        fc = _sel_dot(tri, x_ref[0, rs, :], 3) + carry
        f_ref[0, rs, :] = fc
        ft_ref[0, :, rs] = fc.T[0:heads, :]
        carry = fc[cl - 1:cl, :]


def _forget_cumsum(logf_pad, *, heads, name):
    bsz, seq, _ = logf_pad.shape
    cl = SSD_CHUNK
    tri = (jnp.arange(cl)[:, None] >= jnp.arange(cl)[None, :]).astype(BF16)
    return pl.pallas_call(
        functools.partial(_cumsum_kernel, seq=seq, heads=heads),
        out_shape=(jax.ShapeDtypeStruct((bsz, seq, LANES), F32), jax.ShapeDtypeStruct((bsz, heads, seq), F32)),
        grid=(bsz,),
        in_specs=[pl.BlockSpec((1, seq, LANES), lambda b: (b, 0, 0)), pl.BlockSpec((cl, cl), lambda b: (0, 0))],
        out_specs=(pl.BlockSpec((1, seq, LANES), lambda b: (b, 0, 0)), pl.BlockSpec((1, heads, seq), lambda b: (b, 0, 0))),
        compiler_params=_cparams("parallel"),
        name=name,
    )(logf_pad, tri)


ATT_BLOCK = 256


def _flash_kernel(q_ref, k_ref, v_ref, f_ref, ft_ref, o_ref, m_ref, l_ref, acc_ref, *, heads):
    qi = pl.program_id(1)
    ki = pl.program_id(2)
    tq = q_ref.shape[1]
    tk = k_ref.shape[1]
    hd = ATT_HEAD_DIM

    @pl.when(ki == 0)
    def _():
        m_ref[...] = jnp.full_like(m_ref, NEG_INF)
        l_ref[...] = jnp.zeros_like(l_ref)
        acc_ref[...] = jnp.zeros_like(acc_ref)

    @pl.when(ki <= qi)
    def _():
        rowp = qi * tq + lax.broadcasted_iota(jnp.int32, (tq, tk), 0)
        colp = ki * tk + lax.broadcasted_iota(jnp.int32, (tq, tk), 1)
        mask = rowp >= colp
        f_q = f_ref[0]
        f_k = ft_ref[0]
        for h in range(heads):
            hs = slice(h * hd, (h + 1) * hd)
            s = _dot_nt(q_ref[0, :, hs], k_ref[0, :, hs])
            s = s + (f_q[:, h:h + 1] - f_k[h:h + 1, :])
            s = jnp.where(mask, s, NEG_INF)
            m_prev = m_ref[:, h:h + 1]
            m_new = jnp.maximum(m_prev, jnp.max(s, axis=-1, keepdims=True))
            p = jnp.exp(s - m_new)
            alpha = jnp.exp(m_prev - m_new)
            l_ref[:, h:h + 1] = alpha * l_ref[:, h:h + 1] + jnp.sum(p, axis=-1, keepdims=True)
            acc_ref[:, hs] = alpha * acc_ref[:, hs] + _dot(p.astype(BF16), v_ref[0, :, hs])
            m_ref[:, h:h + 1] = m_new

    @pl.when(ki == qi)
    def _():
        for h in range(heads):
            hs = slice(h * hd, (h + 1) * hd)
            o_ref[0, :, hs] = (acc_ref[:, hs] / l_ref[:, h:h + 1]).astype(o_ref.dtype)


def _flash_prompt(qb, kb, vb, f, ft, *, name):
    bsz, seq, d = qb.shape
    heads = d // ATT_HEAD_DIM
    t = _tile(seq, ATT_BLOCK)
    nb = seq // t
    return pl.pallas_call(
        functools.partial(_flash_kernel, heads=heads),
        out_shape=jax.ShapeDtypeStruct((bsz, seq, d), BF16),
        grid=(bsz, nb, nb),
        in_specs=[
            pl.BlockSpec((1, t, d), lambda b, i, j: (b, i, 0)),
            pl.BlockSpec((1, t, d), lambda b, i, j: (b, jnp.minimum(i, j), 0)),
            pl.BlockSpec((1, t, d), lambda b, i, j: (b, jnp.minimum(i, j), 0)),
            pl.BlockSpec((1, t, LANES), lambda b, i, j: (b, i, 0)),
            pl.BlockSpec((1, heads, t), lambda b, i, j: (b, 0, jnp.minimum(i, j))),
        ],
        out_specs=pl.BlockSpec((1, t, d), lambda b, i, j: (b, i, 0)),
        scratch_shapes=[pltpu.VMEM((t, LANES), F32), pltpu.VMEM((t, LANES), F32), pltpu.VMEM((t, d), F32)],
        compiler_params=_cparams("parallel", "parallel", "arbitrary"),
        name=name,
    )(qb, kb, vb, f, ft)


PAGES_PER_STEP = 4
NEW_PAD = SUBLANES


def _decode_kernel(pt_ref, q_ref, kn_ref, vn_ref, fn_ref, *rest, heads, tnew, pps):
    del pt_ref
    k_refs = rest[0:pps]
    v_refs = rest[pps:2 * pps]
    lf_refs = rest[2 * pps:3 * pps]
    su_ref = rest[3 * pps]
    o_ref = rest[3 * pps + 1]
    qbd_ref, m_ref, l_ref, acc_ref, carry_ref, cncol_ref, pad_ref = rest[3 * pps + 2:]
    j = pl.program_id(1)
    d = q_ref.shape[2]
    hd = ATT_HEAD_DIM
    nr = tnew * heads
    page = k_refs[0].shape[1]

    def head_mask(shape):
        r = lax.broadcasted_iota(jnp.int32, shape, 0)
        ln = lax.broadcasted_iota(jnp.int32, shape, 1)
        return (ln // hd) == (r % heads)

    def update(s, vb):
        m_prev = m_ref[...]
        m_new = jnp.maximum(m_prev, jnp.max(s, axis=-1, keepdims=True))
        p = jnp.exp(s - m_new)
        alpha = jnp.exp(m_prev - m_new)
        l_ref[...] = alpha * l_ref[...] + jnp.sum(p, axis=-1, keepdims=True)
        acc_ref[...] = alpha * acc_ref[...] + _dot(p.astype(BF16), vb)
        m_ref[...] = m_new

    @pl.when(j == 0)
    def _init():
        q = q_ref[0].astype(F32)
        for t in range(tnew):
            qt = jnp.broadcast_to(q[t:t + 1, :], (heads, d))
            qbd_ref[t * heads:(t + 1) * heads, :] = jnp.where(head_mask((heads, d)), qt, 0.0).astype(BF16)
        pad_ref[...] = jnp.zeros_like(pad_ref)
        run = jnp.zeros((1, LANES), F32)
        for t in range(tnew):
            run = run + fn_ref[0, t:t + 1, :]
            pad_ref[t:t + 1, :] = run
        cn_t = pad_ref[...].T
        cn_cols = [cn_t[0:heads, t:t + 1] for t in range(tnew)]
        cncol = jnp.concatenate(cn_cols, axis=0)
        cncol_ref[...] = jnp.broadcast_to(cncol, cncol_ref.shape)
        pad_k = jnp.zeros((NEW_PAD, d), F32)
        kn = jnp.concatenate([kn_ref[0], pad_k[tnew:, :]], axis=0).astype(BF16)
        vn = jnp.concatenate([vn_ref[0], pad_k[tnew:, :]], axis=0).astype(BF16)
        s = _dot_nt(qbd_ref[...], kn)
        cn_tile = jnp.concatenate([cn_t[0:heads, :]] * tnew, axis=0)
        s = s + (cncol - cn_tile)
        r = lax.broadcasted_iota(jnp.int32, s.shape, 0)
        col = lax.broadcasted_iota(jnp.int32, s.shape, 1)
        s = jnp.where(col <= r // heads, s, NEG_INF)
        m_ref[...] = jnp.full_like(m_ref, NEG_INF)
        l_ref[...] = jnp.zeros_like(l_ref)
        acc_ref[...] = jnp.zeros_like(acc_ref)
        carry_ref[...] = jnp.zeros_like(carry_ref)
        update(s, vn)

    su = su_ref[...]
    qbd = qbd_ref[...]
    cncol = cncol_ref[:, 0:1]
    for i in range(pps):
        lf = lf_refs[i][0]
        suf = _sel_dot(su, lf, 3)
        carry = carry_ref[0:1, 0:heads]
        suf_t = (suf + carry).T
        carry_ref[0:1, 0:heads] = carry + suf[0:1, :] + lf[0:1, :]
        bias = jnp.concatenate([suf_t] * tnew, axis=0) + cncol
        s = _dot_nt(qbd, k_refs[i][0].astype(BF16)) + bias
        update(s, v_refs[i][0].astype(BF16))

    @pl.when(j == pl.num_programs(1) - 1)
    def _fin():
        o = acc_ref[...] / l_ref[...]
        for t in range(tnew):
            ot = o[t * heads:(t + 1) * heads, :]
            ot = jnp.where(head_mask((heads, d)), ot, 0.0)
            o_ref[0, t:t + 1, :] = jnp.sum(ot, axis=0, keepdims=True).astype(o_ref.dtype)


def _decode_attention(qb, k_new, v_new, f_new, cache_k, cache_v, cache_logf, page_table, *, name):
    bsz, tnew, d = qb.shape
    heads = d // ATT_HEAD_DIM
    n_pool, page = cache_k.shape[0], cache_k.shape[1]
    n_pages = page_table.shape[1]
    pps = PAGES_PER_STEP
    assert n_pages % pps == 0 and tnew <= NEW_PAD
    ck = cache_k.reshape(n_pool, page, d)
    cv = cache_v.reshape(n_pool, page, d)
    su = (jnp.arange(page)[None, :] > jnp.arange(page)[:, None]).astype(BF16)
    nr = tnew * heads

    def page_spec(width, i):
        return pl.BlockSpec((1, page, width), lambda b, j, pt: (pt[b, n_pages - 1 - (j * pps + i)], 0, 0))

    per_b = lambda shape: pl.BlockSpec(shape, lambda b, j, pt: (b, 0, 0))
    in_specs = [per_b((1, tnew, d)), per_b((1, tnew, d)), per_b((1, tnew, d)), per_b((1, tnew, LANES))]
    in_specs += [page_spec(d, i) for i in range(pps)]
    in_specs += [page_spec(d, i) for i in range(pps)]
    in_specs += [page_spec(heads, i) for i in range(pps)]
    in_specs.append(pl.BlockSpec((page, page), lambda b, j, pt: (0, 0)))
    grid_spec = pltpu.PrefetchScalarGridSpec(
        num_scalar_prefetch=1,
        grid=(bsz, n_pages // pps),
        in_specs=in_specs,
        out_specs=per_b((1, tnew, d)),
        scratch_shapes=[
            pltpu.VMEM((nr, d), BF16),
            pltpu.VMEM((nr, 1), F32),
            pltpu.VMEM((nr, 1), F32),
            pltpu.VMEM((nr, d), F32),
            pltpu.VMEM((SUBLANES, LANES), F32),
            pltpu.VMEM((nr, LANES), F32),
            pltpu.VMEM((NEW_PAD, LANES), F32),
        ],
    )
    return pl.pallas_call(
        functools.partial(_decode_kernel, heads=heads, tnew=tnew, pps=pps),
        out_shape=jax.ShapeDtypeStruct((bsz, tnew, d), BF16),
        grid_spec=grid_spec,
        compiler_params=_cparams("parallel", "arbitrary"),
        name=name,
    )(page_table, qb, k_new, v_new, f_new, *([ck] * pps), *([cv] * pps), *([cache_logf] * pps), su)


def _run(x, p, conv0, ssm0, w, tag, attend):
    bsz, seq, d = x.shape
    m = bsz * seq
    x2 = x.reshape(m, d)
    xb = x2.astype(BF16)
    d_inner = w["d_inner"]
    heads = d // ATT_HEAD_DIM

    zx = _proj(xb, w["w_zx"], name=f"in_proj_{tag}")
    dt_raw = _proj(xb, w["w_dt"], name=f"dt_proj_{tag}")
    act, new_conv = _conv_silu(zx.reshape(bsz, seq, -1), conv0, w["conv_w"], w["conv_b"], d_inner=d_inner,
                               name=f"conv_{tag}")
    g, new_ssm = _ssd(zx.reshape(bsz, seq, -1), act, dt_raw.reshape(bsz, seq, LANES), ssm0, w["dt_bias"], w["a_log"],
                      w["d_skip"], w["norm_w"], d_inner=d_inner, name=f"ssd_{tag}")
    x2, xb = _mm_ln(g.reshape(m, d_inner), w["w_out"], x2, w["ln1_g"][0], w["ln1_b"][0], name=f"out_proj_ln_{tag}")
    x2, xb = _ffn(x2, xb, w["wg"][0], w["wu"][0], w["wd"][0], w["ln2_g"][0], w["ln2_b"][0], name=f"ffn0_{tag}")
    x2, xb = _ple(x2, xb, p[0].reshape(m, -1), w["ple_g"][0], w["ple_p"][0], name=f"ple0_{tag}")

    k, kb = _proj(xb, w["w_k"], mode="dual", name=f"k_proj_{tag}")
    v, vb = _proj(xb, w["w_v"], mode="dual", name=f"v_proj_{tag}")
    logf_pad = _proj(xb, w["w_f"], mode="logsig", bias=w["b_f"], name=f"f_proj_{tag}")

    qb = _proj(xb, w["w_q"], mode="scale", scale=ATT_HEAD_DIM ** -0.5, out_dtype=BF16, name=f"q_proj_{tag}")
    o = attend(qb.reshape(bsz, seq, d), k.reshape(bsz, seq, d), v.reshape(bsz, seq, d), kb.reshape(bsz, seq, d),
               vb.reshape(bsz, seq, d), logf_pad.reshape(bsz, seq, LANES))
    x2, xb = _mm_ln(o.reshape(m, d), w["w_o"], x2, w["ln1_g"][1], w["ln1_b"][1], name=f"o_proj_ln_{tag}")
    x2, xb = _ffn(x2, xb, w["wg"][1], w["wu"][1], w["wd"][1], w["ln2_g"][1], w["ln2_b"][1], name=f"ffn1_{tag}")
    y, _ = _ple(x2, xb, p[1].reshape(m, -1), w["ple_g"][1], w["ple_p"][1], name=f"ple1_{tag}")

    return (y.reshape(bsz, seq, d), new_conv[None], new_ssm[None],
            k.reshape(bsz, seq, heads, ATT_HEAD_DIM), v.reshape(bsz, seq, heads, ATT_HEAD_DIM),
            logf_pad.reshape(bsz, seq, LANES)[..., :heads])


def kernel(x_prompt, x_sample, p_prompt, p_sample, state_conv, state_ssm, cache_k, cache_v, cache_logf, page_table, a_w_in, a_conv_w, a_conv_b, a_dt_bias, a_log, a_d, a_norm_w, a_w_out, kv_w, kv_b_f, b_w_q, b_w_o, ln1_g, ln1_b, ffn_w_gate, ffn_w_up, ffn_w_down, ln2_g, ln2_b, ple_w_gate, ple_w_proj):
    assert a_w_in.shape[0] == 1 and b_w_q.shape[0] == 1, "one SSD layer followed by one attention layer"
    d = x_prompt.shape[-1]
    heads = d // ATT_HEAD_DIM
    d_att = heads * ATT_HEAD_DIM
    conv_dim = a_conv_w.shape[-1]
    d_inner = conv_dim - 2 * SSM_GROUPS * SSM_STATE
    ssm_heads = d_inner // SSM_HEAD_DIM
    zx_dim = d_inner + conv_dim

    def pad_cols(wm):
        return jnp.pad(wm, ((0, 0), (0, LANES - wm.shape[1])))

    bf = lambda t: t.astype(BF16)
    w = dict(
        d_inner=d_inner,
        w_zx=bf(a_w_in[0][:, :zx_dim]), w_dt=bf(pad_cols(a_w_in[0][:, zx_dim:])),
        conv_w=a_conv_w[0], conv_b=a_conv_b[0], dt_bias=a_dt_bias[0], a_log=a_log[0], d_skip=a_d[0],
        norm_w=a_norm_w[0], w_out=bf(a_w_out[0]),
        w_k=bf(kv_w[:, :d_att]), w_v=bf(kv_w[:, d_att:2 * d_att]), w_f=bf(pad_cols(kv_w[:, 2 * d_att:])),
        b_f=jnp.pad(kv_b_f.astype(F32), (0, LANES - heads)).reshape(1, LANES),
        w_q=bf(b_w_q[0]), w_o=bf(b_w_o[0]),
        ln1_g=ln1_g, ln1_b=ln1_b, ln2_g=ln2_g, ln2_b=ln2_b,
        wg=bf(ffn_w_gate), wu=bf(ffn_w_up), wd=bf(ffn_w_down), ple_g=bf(ple_w_gate), ple_p=bf(ple_w_proj),
    )
    assert ssm_heads * SSM_HEAD_DIM == d_inner

    def attend_prompt(qb, k, v, kb, vb, logf_pad):
        f, ft = _forget_cumsum(logf_pad, heads=heads, name="forget_cumsum")
        return _flash_prompt(qb, kb, vb, f, ft, name="flash_prompt")

    def attend_sample(qb, k, v, kb, vb, logf_pad):
        return _decode_attention(qb, k, v, logf_pad, cache_k, cache_v, cache_logf, page_table, name="decode_attn")

    bp = x_prompt.shape[0]
    conv0_p = jnp.zeros((bp, CONV_WIDTH - 1, conv_dim), F32)
    outs_p = _run(x_prompt, p_prompt, conv0_p, None, w, "p", attend_prompt)
    outs_s = _run(x_sample, p_sample, state_conv[0], state_ssm[0], w, "s", attend_sample)
    y_p, conv_p, ssm_p, k_p, v_p, f_p = outs_p
    y_s, conv_s, ssm_s, k_s, v_s, f_s = outs_s
    return (y_p, y_s, conv_p, ssm_p, k_p, v_p, f_p, conv_s, ssm_s, k_s, v_s, f_s)
```

```python
import functools

import jax
import jax.numpy as jnp
from jax import lax
from jax.experimental import pallas as pl
from jax.experimental.pallas import tpu as pltpu

F32 = jnp.float32
BF16 = jnp.bfloat16

DEPTH = 2
SSM_HEAD_DIM = 64
SSM_GROUPS = 8
SSM_STATE = 128
CONV_WIDTH = 4
SSD_CHUNK = 128
ATT_HEAD_DIM = 128
ALPHA = (2 * DEPTH) ** 0.25
LN_EPS = 1e-5
RMS_EPS = 1e-5
NEG_INF = -1e30

LANES = 128
SUBLANES = 8
BF16_ROWS = 2 * SUBLANES
V7X_VMEM_BYTES = 64 * 1024 * 1024
VMEM_LIMIT = V7X_VMEM_BYTES * 7 // 8


def _cparams(*sem):
    return pltpu.CompilerParams(dimension_semantics=sem, vmem_limit_bytes=VMEM_LIMIT)


def _dot(a, b):
    return jnp.dot(a, b, preferred_element_type=F32)


def _dot_nt(a, b):
    return lax.dot_general(a, b, (((1,), (1,)), ((), ())), preferred_element_type=F32)


def _split_bf16(x, parts):
    out = []
    r = x
    for i in range(parts):
        p = r.astype(BF16)
        out.append(p)
        if i + 1 < parts:
            r = r - p.astype(F32)
    return out


def _sel_dot(sel, x, parts):
    acc = None
    for p in _split_bf16(x, parts):
        t = _dot(sel, p)
        acc = t if acc is None else acc + t
    return acc


def _dot_sel(x, sel, parts):
    acc = None
    for p in _split_bf16(x, parts):
        t = _dot(p, sel)
        acc = t if acc is None else acc + t
    return acc


def _softplus(x):
    return jnp.maximum(x, 0.0) + jnp.log1p(jnp.exp(-jnp.abs(x)))


def _log_sigmoid(x):
    return jnp.minimum(x, 0.0) - jnp.log1p(jnp.exp(-jnp.abs(x)))


def _layer_norm(v, g, b):
    mu = jnp.mean(v, axis=-1, keepdims=True)
    d = v - mu
    var = jnp.mean(d * d, axis=-1, keepdims=True)
    return d * lax.rsqrt(var + LN_EPS) * g + b


def _tile(n, cap):
    t = min(n, cap)
    assert n % t == 0, (n, cap)
    return t


RESIDENT_WEIGHT_BYTES = V7X_VMEM_BYTES // 4


def _resident(shape, index_map):
    return pl.BlockSpec(shape, index_map, pipeline_mode=pl.Buffered(1))


def _proj_kernel(*refs, mode, scale):
    if mode == "logsig":
        x_ref, w_ref, b_ref, o_ref = refs
    elif mode == "dual":
        x_ref, w_ref, o_ref, ob_ref = refs
    else:
        x_ref, w_ref, o_ref = refs
    acc = _dot(x_ref[...], w_ref[...])
    if mode == "logsig":
        o_ref[...] = _log_sigmoid(acc + b_ref[...])
    elif mode == "dual":
        o_ref[...] = acc
        ob_ref[...] = acc.astype(BF16)
    elif mode == "scale":
        o_ref[...] = (acc * scale).astype(o_ref.dtype)
    else:
        o_ref[...] = acc.astype(o_ref.dtype)


def _proj(xb, w, *, mode="plain", out_dtype=F32, scale=1.0, bias=None, name):
    m, k = xb.shape
    n = w.shape[1]
    tm = _tile(m, 1024)
    resident = k * n * w.dtype.itemsize <= RESIDENT_WEIGHT_BYTES
    tn = n if resident else _tile(n, 1024)
    in_specs = [pl.BlockSpec((tm, k), lambda i, j: (i, 0)),
                _resident((k, tn), lambda i, j: (0, j)) if resident else pl.BlockSpec((k, tn), lambda i, j: (0, j))]
    args = [xb, w]
    o_spec = pl.BlockSpec((tm, tn), lambda i, j: (i, j))
    if mode == "logsig":
        in_specs.append(pl.BlockSpec((1, tn), lambda i, j: (0, j)))
        args.append(bias)
    if mode == "dual":
        out_shape = (jax.ShapeDtypeStruct((m, n), F32), jax.ShapeDtypeStruct((m, n), BF16))
        out_specs = (o_spec, o_spec)
    else:
        out_shape = jax.ShapeDtypeStruct((m, n), out_dtype)
        out_specs = o_spec
    return pl.pallas_call(
        functools.partial(_proj_kernel, mode=mode, scale=scale),
        out_shape=out_shape,
        grid=(m // tm, n // tn),
        in_specs=in_specs,
        out_specs=out_specs,
        compiler_params=_cparams("parallel", "arbitrary"),
        name=name,
    )(*args)


CONV_PAD = SUBLANES
CONV_ROWS = 256


def _conv_kernel(x_ref, st_ref, w_ref, b_ref, act_ref, nc_ref, pad_ref, *, seq):
    lo = CONV_PAD - (CONV_WIDTH - 1)
    pad_ref[lo:CONV_PAD, :] = st_ref[0]
    pad_ref[CONV_PAD:CONV_PAD + seq, :] = x_ref[0]
    w = w_ref[...]
    b = b_ref[...]
    rc = min(seq, CONV_ROWS)
    for r0 in range(0, seq, rc):
        acc = b + w[0:1, :] * pad_ref[lo + r0:lo + r0 + rc, :]
        for k in range(1, CONV_WIDTH):
            acc = acc + w[k:k + 1, :] * pad_ref[lo + k + r0:lo + k + r0 + rc, :]
        act_ref[0, r0:r0 + rc, :] = acc * jax.nn.sigmoid(acc)
    nc_ref[0] = pad_ref[lo + seq:CONV_PAD + seq, :]


def _conv_silu(zx, conv_state, conv_w, conv_b, *, d_inner, name):
    bsz, seq, _ = zx.shape
    conv_dim = conv_w.shape[1]
    tc = 512 if seq >= SSD_CHUNK else 2048
    assert conv_dim % tc == 0 and d_inner % tc == 0
    off = d_inner // tc
    return pl.pallas_call(
        functools.partial(_conv_kernel, seq=seq),
        out_shape=(jax.ShapeDtypeStruct((bsz, seq, conv_dim), F32),
                   jax.ShapeDtypeStruct((bsz, CONV_WIDTH - 1, conv_dim), F32)),
        grid=(bsz, conv_dim // tc),
        in_specs=[
            pl.BlockSpec((1, seq, tc), lambda b, j: (b, 0, off + j)),
            pl.BlockSpec((1, CONV_WIDTH - 1, tc), lambda b, j: (b, 0, j)),
            pl.BlockSpec((CONV_WIDTH, tc), lambda b, j: (0, j)),
            pl.BlockSpec((1, tc), lambda b, j: (0, j)),
        ],
        out_specs=(
            pl.BlockSpec((1, seq, tc), lambda b, j: (b, 0, j)),
            pl.BlockSpec((1, CONV_WIDTH - 1, tc), lambda b, j: (b, 0, j)),
        ),
        scratch_shapes=[pltpu.VMEM((CONV_PAD + seq, tc), F32)],
        compiler_params=_cparams("parallel", "parallel"),
        name=name,
    )(zx, conv_state, conv_w, conv_b.reshape(1, conv_dim))


def _ssd_kernel(*refs, rows, cl, has_h0, d_inner, n_state):
    groups = SSM_GROUPS
    gw = d_inner // groups
    pairs = gw // LANES
    heads_per_group = gw // SSM_HEAD_DIM
    padded = rows < cl
    it = iter(refs)
    xs_ref, bm_ref, cm_ref, z_ref, dt_ref = next(it), next(it), next(it), next(it), next(it)
    dtb_ref, alog_ref, dsk_ref, nw_ref, tri_ref, exp_ref = (next(it), next(it), next(it), next(it), next(it), next(it))
    h0_ref = next(it) if has_h0 else None
    g_ref, hout_ref = next(it), next(it)
    ht_ref = next(it)
    if padded:
        xs_p, bm_p, cm_p, z_p, dt_p = next(it), next(it), next(it), next(it), next(it)

    c = pl.program_id(1)

    @pl.when(c == 0)
    def _init():
        if has_h0:
            for i in range(d_inner // LANES):
                ht_ref[:, i * LANES:(i + 1) * LANES] = h0_ref[0, i * LANES:(i + 1) * LANES, :].T
        else:
            ht_ref[...] = jnp.zeros_like(ht_ref)

    if padded:
        for src, dst in ((xs_ref, xs_p), (bm_ref, bm_p), (cm_ref, cm_p), (z_ref, z_p), (dt_ref, dt_p)):
            dst[...] = jnp.zeros_like(dst)
            dst[0:rows, :] = src[0]
        xs_v, bm_v, cm_v, z_v, dt_v = xs_p, bm_p, cm_p, z_p, dt_p
        rd = lambda ref, sl: ref[:, sl]
    else:
        xs_v, bm_v, cm_v, z_v, dt_v = xs_ref, bm_ref, cm_ref, z_ref, dt_ref
        rd = lambda ref, sl: ref[0, :, sl]

    full = slice(None)
    dt = _softplus(rd(dt_v, full) + dtb_ref[...])
    if padded:
        row = lax.broadcasted_iota(jnp.int32, dt.shape, 0)
        dt = jnp.where(row < rows, dt, 0.0)
    a = -jnp.exp(alog_ref[...])
    tri = tri_ref[...]
    a_cum = _sel_dot(tri, dt * a, 3)
    a_cum_t = a_cum.T
    a_last = a_cum[cl - 1:cl, :]
    ea = jnp.exp(a_cum)
    de = jnp.exp(a_last - a_cum)

    li = lax.broadcasted_iota(jnp.int32, (cl, cl), 0)
    si = lax.broadcasted_iota(jnp.int32, (cl, cl), 1)
    causal = li >= si
    lane = lax.broadcasted_iota(jnp.int32, (cl, LANES), 1)
    first_head = lane < SSM_HEAD_DIM

    for g in range(groups):
        gs = slice(g * gw, (g + 1) * gw)
        ns = slice(g * n_state, (g + 1) * n_state)
        expand = exp_ref[:, gs]
        xs_g = rd(xs_v, gs)
        dt_e = _dot_sel(dt, expand, 2)
        ea_e = _dot_sel(ea, expand, 2)
        de_e = _dot_sel(de, expand, 2)
        xdt = xs_g * dt_e
        bm_t = rd(bm_v, ns).T.astype(BF16)
        cm_b = rd(cm_v, ns).astype(BF16)
        cb = _dot(cm_b, bm_t)
        ht_g = ht_ref[:, gs]
        y_off = _dot(cm_b, ht_g.astype(BF16)) * ea_e
        y_parts = []
        for j in range(pairs):
            sc = []
            for e in range(2):
                h = g * heads_per_group + 2 * j + e
                seg = a_cum[:, h:h + 1] - a_cum_t[h:h + 1, :]
                dec = jnp.exp(jnp.where(causal, seg, -jnp.inf))
                sc.append((cb * dec).astype(BF16))
            lhs = jnp.concatenate(sc, axis=1)
            xp = xdt[:, j * LANES:(j + 1) * LANES]
            rhs = jnp.concatenate([jnp.where(first_head, xp, 0.0), jnp.where(first_head, 0.0, xp)],
                                  axis=0).astype(BF16)
            y_parts.append(_dot(lhs, rhs))
        y_g = jnp.concatenate(y_parts, axis=1) + y_off + dsk_ref[:, gs] * xs_g
        ht_ref[:, gs] = ea_e[cl - 1:cl, :] * ht_g + _dot(bm_t, (xdt * de_e).astype(BF16))
        z_g = rd(z_v, gs)
        gz = y_g * (z_g * jax.nn.sigmoid(z_g))
        ms = jnp.mean(gz * gz, axis=-1, keepdims=True)
        out_g = gz * lax.rsqrt(ms + RMS_EPS) * nw_ref[:, gs]
        g_ref[0, :, gs] = out_g[0:rows, :].astype(BF16)

    @pl.when(c == pl.num_programs(1) - 1)
    def _fin():
        for i in range(d_inner // LANES):
            hout_ref[0, i * LANES:(i + 1) * LANES, :] = ht_ref[:, i * LANES:(i + 1) * LANES].T


def _ssd(zx, act, dt_raw, ssm0, dt_bias, a_log, d_skip, norm_w, *, d_inner, name):
    bsz, seq, _ = act.shape
    n_state = SSM_STATE
    heads = d_inner // SSM_HEAD_DIM
    gn = SSM_GROUPS * n_state
    assert heads <= LANES and d_inner % gn == 0 and gn % LANES == 0
    if seq % SSD_CHUNK == 0:
        rows = cl = SSD_CHUNK
    else:
        rows, cl = seq, -(-seq // BF16_ROWS) * BF16_ROWS
    nc = seq // rows
    has_h0 = ssm0 is not None

    def pad_heads(v):
        return jnp.pad(v.astype(F32), (0, LANES - heads)).reshape(1, LANES)

    tri = (jnp.arange(cl)[:, None] >= jnp.arange(cl)[None, :]).astype(BF16)
    expand = (jnp.arange(LANES)[:, None] == (jnp.arange(d_inner) // SSM_HEAD_DIM)[None, :]).astype(BF16)
    dsk = jnp.repeat(d_skip.astype(F32), SSM_HEAD_DIM).reshape(1, d_inner)
    nw = norm_w.astype(F32).reshape(1, d_inner)
    bc_off = d_inner // gn

    const = lambda shape: pl.BlockSpec(shape, lambda b, c: (0,) * len(shape))
    in_specs = [
        pl.BlockSpec((1, rows, d_inner), lambda b, c: (b, c, 0)),
        pl.BlockSpec((1, rows, gn), lambda b, c: (b, c, bc_off)),
        pl.BlockSpec((1, rows, gn), lambda b, c: (b, c, bc_off + 1)),
        pl.BlockSpec((1, rows, d_inner), lambda b, c: (b, c, 0)),
        pl.BlockSpec((1, rows, LANES), lambda b, c: (b, c, 0)),
        const((1, LANES)), const((1, LANES)), const((1, d_inner)), const((1, d_inner)),
        const((cl, cl)), const((LANES, d_inner)),
    ]
    args = [act, act, act, zx, dt_raw, pad_heads(dt_bias), pad_heads(a_log), dsk, nw, tri, expand]
    if has_h0:
        in_specs.append(pl.BlockSpec((1, d_inner, n_state), lambda b, c: (b, 0, 0)))
        args.append(ssm0.reshape(bsz, d_inner, n_state))
    scratch = [pltpu.VMEM((n_state, d_inner), F32)]
    if rows < cl:
        scratch += [pltpu.VMEM((cl, d_inner), F32), pltpu.VMEM((cl, gn), F32), pltpu.VMEM((cl, gn), F32),
                    pltpu.VMEM((cl, d_inner), F32), pltpu.VMEM((cl, LANES), F32)]
    g, hout = pl.pallas_call(
        functools.partial(_ssd_kernel, rows=rows, cl=cl, has_h0=has_h0, d_inner=d_inner, n_state=n_state),
        out_shape=(jax.ShapeDtypeStruct((bsz, seq, d_inner), BF16),
                   jax.ShapeDtypeStruct((bsz, d_inner, n_state), F32)),
        grid=(bsz, nc),
        in_specs=in_specs,
        out_specs=(pl.BlockSpec((1, rows, d_inner), lambda b, c: (b, c, 0)),
                   pl.BlockSpec((1, d_inner, n_state), lambda b, c: (b, 0, 0))),
        scratch_shapes=scratch,
        compiler_params=_cparams("parallel", "arbitrary"),
        name=name,
    )(*args)
    return g, hout.reshape(bsz, heads, SSM_HEAD_DIM, n_state)


def _mm_ln_kernel(x_ref, w_ref, res_ref, g_ref, b_ref, o_ref, ob_ref):
    y = _layer_norm(ALPHA * res_ref[...] + _dot(x_ref[...], w_ref[...]), g_ref[...], b_ref[...])
    o_ref[...] = y
    ob_ref[...] = y.astype(BF16)


def _mm_ln(xb, w, res, gamma, beta, *, name):
    m, kdim = xb.shape
    n = w.shape[1]
    assert kdim * n * w.dtype.itemsize <= RESIDENT_WEIGHT_BYTES
    tm = _tile(m, 512)
    row = lambda shape: pl.BlockSpec(shape, lambda i: (i, 0))
    return pl.pallas_call(
        _mm_ln_kernel,
        out_shape=(jax.ShapeDtypeStruct((m, n), F32), jax.ShapeDtypeStruct((m, n), BF16)),
        grid=(m // tm,),
        in_specs=[
            row((tm, kdim)),
            _resident((kdim, n), lambda i: (0, 0)),
            row((tm, n)),
            _resident((1, n), lambda i: (0, 0)),
            _resident((1, n), lambda i: (0, 0)),
        ],
        out_specs=(row((tm, n)), row((tm, n))),
        compiler_params=_cparams("parallel"),
        name=name,
    )(xb, w, res, gamma.reshape(1, n), beta.reshape(1, n))


def _ffn_kernel(x_ref, xb_ref, wg_ref, wu_ref, wd_ref, g_ref, b_ref, o_ref, ob_ref, acc_ref):
    f = pl.program_id(1)

    @pl.when(f == 0)
    def _():
        acc_ref[...] = jnp.zeros_like(acc_ref)

    xb = xb_ref[...]
    a = _dot(xb, wg_ref[...])
    u = _dot(xb, wu_ref[...])
    h = (a * jax.nn.sigmoid(a) * u).astype(BF16)
    acc_ref[...] += _dot(h, wd_ref[...])

    @pl.when(f == pl.num_programs(1) - 1)
    def _():
        y = _layer_norm(ALPHA * x_ref[...] + acc_ref[...], g_ref[...], b_ref[...])
        o_ref[...] = y
        ob_ref[...] = y.astype(BF16)


def _ffn(x, xb, wg, wu, wd, gamma, beta, *, name):
    m, d = x.shape
    dff = wg.shape[1]
    tm = _tile(m, 512)
    tf = _tile(dff, 512)
    row = lambda shape: pl.BlockSpec(shape, lambda i, f: (i, 0))
    vec = pl.BlockSpec((1, d), lambda i, f: (0, 0))
    return pl.pallas_call(
        _ffn_kernel,
        out_shape=(jax.ShapeDtypeStruct((m, d), F32), jax.ShapeDtypeStruct((m, d), BF16)),
        grid=(m // tm, dff // tf),
        in_specs=[
            row((tm, d)), row((tm, d)),
            pl.BlockSpec((d, tf), lambda i, f: (0, f)),
            pl.BlockSpec((d, tf), lambda i, f: (0, f)),
            pl.BlockSpec((tf, d), lambda i, f: (f, 0)),
            vec, vec,
        ],
        out_specs=(row((tm, d)), row((tm, d))),
        scratch_shapes=[pltpu.VMEM((tm, d), F32)],
        compiler_params=_cparams("parallel", "arbitrary"),
        name=name,
    )(x, xb, wg, wu, wd, gamma.reshape(1, d), beta.reshape(1, d))


def _ple_kernel(x_ref, xb_ref, p_ref, wg_ref, wp_ref, o_ref, ob_ref):
    gate = _dot(xb_ref[...], wg_ref[...])
    proj = _dot(p_ref[...].astype(BF16), wp_ref[...])
    y = x_ref[...] + jax.nn.sigmoid(gate) * proj
    o_ref[...] = y
    ob_ref[...] = y.astype(BF16)


def _ple(x, xb, p, w_gate, w_proj, *, name):
    m, d = x.shape
    pd = p.shape[1]
    assert d * d * w_gate.dtype.itemsize <= RESIDENT_WEIGHT_BYTES
    tm = _tile(m, 512)
    row = lambda width: pl.BlockSpec((tm, width), lambda i: (i, 0))
    return pl.pallas_call(
        _ple_kernel,
        out_shape=(jax.ShapeDtypeStruct((m, d), F32), jax.ShapeDtypeStruct((m, d), BF16)),
        grid=(m // tm,),
        in_specs=[
            row(d), row(d), row(pd),
            _resident((d, d), lambda i: (0, 0)),
            _resident((pd, d), lambda i: (0, 0)),
        ],
        out_specs=(row(d), row(d)),
        compiler_params=_cparams("parallel"),
        name=name,
    )(x, xb, p, w_gate, w_proj)


def _cumsum_kernel(x_ref, tri_ref, f_ref, ft_ref, *, seq, heads):
    cl = SSD_CHUNK
    tri = tri_ref[...]
    carry = jnp.zeros((1, LANES), F32)
    for c in range(seq // cl):
        rs = slice(c * cl, (c + 1) * cl)
        fc = _sel_dot(tri, x_ref[0, rs, :], 3) + carry
        f_ref[0, rs, :] = fc
        ft_ref[0, :, rs] = fc.T[0:heads, :]
        carry = fc[cl - 1:cl, :]


def _forget_cumsum(logf_pad, *, heads, name):
    bsz, seq, _ = logf_pad.shape
    cl = SSD_CHUNK
    tri = (jnp.arange(cl)[:, None] >= jnp.arange(cl)[None, :]).astype(BF16)
    return pl.pallas_call(
        functools.partial(_cumsum_kernel, seq=seq, heads=heads),
        out_shape=(jax.ShapeDtypeStruct((bsz, seq, LANES), F32), jax.ShapeDtypeStruct((bsz, heads, seq), F32)),
        grid=(bsz,),
        in_specs=[pl.BlockSpec((1, seq, LANES), lambda b: (b, 0, 0)), pl.BlockSpec((cl, cl), lambda b: (0, 0))],
        out_specs=(pl.BlockSpec((1, seq, LANES), lambda b: (b, 0, 0)), pl.BlockSpec((1, heads, seq), lambda b: (b, 0, 0))),
        compiler_params=_cparams("parallel"),
        name=name,
    )(logf_pad, tri)


ATT_BLOCK = 256


def _flash_kernel(q_ref, k_ref, v_ref, f_ref, ft_ref, o_ref, m_ref, l_ref, acc_ref, fq_ref, *, heads):
    qi = pl.program_id(1)
    ki = pl.program_id(2)
    tq = q_ref.shape[1]
    tk = k_ref.shape[1]
    hd = ATT_HEAD_DIM
    rep = tk // LANES

    def lanes_to_tk(v):
        return jnp.concatenate([v] * rep, axis=1)

    @pl.when(ki == 0)
    def _():
        m_ref[...] = jnp.full_like(m_ref, NEG_INF)
        l_ref[...] = jnp.zeros_like(l_ref)
        acc_ref[...] = jnp.zeros_like(acc_ref)
        f_q = f_ref[0]
        for h in range(heads):
            fq_ref[h] = jnp.broadcast_to(f_q[:, h:h + 1], (tq, LANES))

    def step(diagonal):
        f_k = ft_ref[0]
        if diagonal:
            mask = lax.broadcasted_iota(jnp.int32, (tq, tk), 0) >= lax.broadcasted_iota(jnp.int32, (tq, tk), 1)
        for h in range(heads):
            hs = slice(h * hd, (h + 1) * hd)
            s = _dot_nt(q_ref[0, :, hs], k_ref[0, :, hs])
            x = s + (lanes_to_tk(fq_ref[h]) - f_k[h:h + 1, :])
            if diagonal:
                x = jnp.where(mask, x, NEG_INF)
            m_prev = m_ref[h]
            m_new = jnp.maximum(m_prev, jnp.max(x, axis=-1, keepdims=True))
            p = jnp.exp(x - lanes_to_tk(m_new))
            alpha = jnp.exp(m_prev - m_new)
            l_ref[h] = alpha * l_ref[h] + jnp.sum(p, axis=-1, keepdims=True)
            acc_ref[h] = alpha * acc_ref[h] + _dot(p.astype(BF16), v_ref[0, :, hs])
            m_ref[h] = m_new

    @pl.when(ki < qi)
    def _():
        step(False)

    @pl.when(ki == qi)
    def _():
        step(True)
        for h in range(heads):
            o_ref[0, :, h * hd:(h + 1) * hd] = (acc_ref[h] / l_ref[h]).astype(o_ref.dtype)


def _flash_prompt(qb, kb, vb, f, ft, *, name):
    bsz, seq, d = qb.shape
    heads = d // ATT_HEAD_DIM
    t = _tile(seq, ATT_BLOCK)
    nb = seq // t
    return pl.pallas_call(
        functools.partial(_flash_kernel, heads=heads),
        out_shape=jax.ShapeDtypeStruct((bsz, seq, d), BF16),
        grid=(bsz, nb, nb),
        in_specs=[
            pl.BlockSpec((1, t, d), lambda b, i, j: (b, i, 0)),
            pl.BlockSpec((1, t, d), lambda b, i, j: (b, jnp.minimum(i, j), 0)),
            pl.BlockSpec((1, t, d), lambda b, i, j: (b, jnp.minimum(i, j), 0)),
            pl.BlockSpec((1, t, LANES), lambda b, i, j: (b, i, 0)),
            pl.BlockSpec((1, heads, t), lambda b, i, j: (b, 0, jnp.minimum(i, j))),
        ],
        out_specs=pl.BlockSpec((1, t, d), lambda b, i, j: (b, i, 0)),
        scratch_shapes=[pltpu.VMEM((heads, t, LANES), F32), pltpu.VMEM((heads, t, LANES), F32),
                        pltpu.VMEM((heads, t, ATT_HEAD_DIM), F32), pltpu.VMEM((heads, t, LANES), F32)],
        compiler_params=_cparams("parallel", "parallel", "arbitrary"),
        name=name,
    )(qb, kb, vb, f, ft)


PAGES_PER_STEP = 4


def _decode_kernel(pt_ref, q_ref, kn_ref, vn_ref, fn_ref, *rest, heads, tnew, pps):
    del pt_ref
    k_refs = rest[0:pps]
    v_refs = rest[pps:2 * pps]
    lf_refs = rest[2 * pps:3 * pps]
    a_ref, a2_ref, sur_ref, o_ref = rest[3 * pps:3 * pps + 4]
    m_ref, l_ref, acc_ref, carry_ref, cnrep_ref, madd_ref, cnt_ref = rest[3 * pps + 4:]
    j = pl.program_id(1)
    hd = ATT_HEAD_DIM
    page = k_refs[0].shape[1]
    nq = tnew * heads
    nk = page * heads
    fr = nk // LANES

    def widen(v, w):
        return v[:, 0:w] if w <= LANES else jnp.concatenate([v] * (w // LANES), axis=1)

    def update(xs, vbs):
        m_prev = m_ref[...]
        m_new = m_prev
        for x in xs:
            m_new = jnp.maximum(m_new, jnp.max(x, axis=-1, keepdims=True))
        alpha = jnp.exp(m_prev - m_new)
        l_new = alpha * l_ref[...]
        acc = alpha * acc_ref[...]
        for x, vb in zip(xs, vbs):
            p = jnp.exp(x - widen(m_new, x.shape[1]))
            l_new = l_new + jnp.sum(p, axis=-1, keepdims=True)
            acc = acc + _dot(p.astype(BF16), vb)
        l_ref[...] = l_new
        acc_ref[...] = acc
        m_ref[...] = m_new

    @pl.when(j == 0)
    def _init():
        m_ref[...] = jnp.full_like(m_ref, NEG_INF)
        l_ref[...] = jnp.zeros_like(l_ref)
        acc_ref[...] = jnp.zeros_like(acc_ref)
        carry_ref[...] = jnp.zeros_like(carry_ref)
        cnt_ref[...] = jnp.zeros_like(cnt_ref)
        run = jnp.zeros((1, LANES), F32)
        for t in range(tnew):
            run = run + fn_ref[0, t:t + 1, :]
            cnt_ref[t:t + 1, :] = run
        cn = cnt_ref[...]
        cn_t = cn.T
        cncol = jnp.concatenate([cn_t[0:heads, t:t + 1] for t in range(tnew)], axis=0)
        cnrep_ref[...] = jnp.broadcast_to(cncol, (nq, LANES))
        r = lax.broadcasted_iota(jnp.int32, (nq, nk), 0)
        c = lax.broadcasted_iota(jnp.int32, (nq, nk), 1)
        madd_ref[...] = jnp.where(r % heads == c % heads, jnp.broadcast_to(cncol, (nq, nk)), NEG_INF)
        s = _dot_nt(q_ref[0], kn_ref[0].astype(BF16))
        brow = jnp.concatenate([cn[t:t + 1, 0:heads] for t in range(tnew)], axis=1)
        x = s + (cnrep_ref[:, 0:nq] - brow)
        rr = lax.broadcasted_iota(jnp.int32, (nq, nq), 0)
        cc = lax.broadcasted_iota(jnp.int32, (nq, nq), 1)
        ok = jnp.logical_and(rr % heads == cc % heads, cc // heads <= rr // heads)
        update([jnp.where(ok, x, NEG_INF)], [vn_ref[0].astype(BF16)])

    a_sel = a_ref[...]
    a2_sel = a2_ref[...]
    sur = sur_ref[...]
    q = q_ref[0]
    xs, vbs = [], []
    for i in range(pps):
        lf = lf_refs[i][0]
        rowtot = _dot_sel(lf, a2_sel, 3)
        suf = _dot_sel(lf, a_sel, 3) + _sel_dot(sur, rowtot, 3) + carry_ref[0:1, :]
        carry_ref[0:1, :] = carry_ref[0:1, :] + jnp.sum(rowtot, axis=0, keepdims=True)
        brow = jnp.concatenate([suf[rw:rw + 1, :] for rw in range(fr)], axis=1)
        k2 = k_refs[i][0].reshape(nk, hd).astype(BF16)
        xs.append(_dot_nt(q, k2) + (brow + madd_ref[...]))
        vbs.append(v_refs[i][0].reshape(nk, hd).astype(BF16))
    update(xs, vbs)

    @pl.when(j == pl.num_programs(1) - 1)
    def _fin():
        o_ref[0] = acc_ref[...] / l_ref[...]


def _decode_attention(qb, k_new, v_new, f_new, cache_k, cache_v, cache_logf, page_table, *, name):
    bsz, tnew, d = qb.shape
    n_pool, page, heads, hd = cache_k.shape
    n_pages = page_table.shape[1]
    pps = PAGES_PER_STEP
    nq = tnew * heads
    nk = page * heads
    fr = nk // LANES
    assert n_pages % pps == 0 and tnew <= SUBLANES and hd == ATT_HEAD_DIM and heads * hd == d
    assert LANES % heads == 0 and nk % LANES == 0
    lane = jnp.arange(LANES)
    same_head = (lane[:, None] % heads) == (lane[None, :] % heads)
    a2_sel = same_head.astype(BF16)
    a_sel = jnp.logical_and(same_head, lane[:, None] // heads > lane[None, :] // heads).astype(BF16)
    sur = (jnp.arange(fr)[None, :] > jnp.arange(fr)[:, None]).astype(BF16)
    lf_flat = cache_logf.reshape(n_pool, fr, LANES)

    def slot(b, j, pt, i):
        return pt[b, n_pages - 1 - (j * pps + i)]

    def kv_spec(i):
        return pl.BlockSpec((1, page, heads, hd), lambda b, j, pt: (slot(b, j, pt, i), 0, 0, 0))

    def lf_spec(i):
        return pl.BlockSpec((1, fr, LANES), lambda b, j, pt: (slot(b, j, pt, i), 0, 0))

    per_b = lambda shape: pl.BlockSpec(shape, lambda b, j, pt: (b, 0, 0))
    const = lambda shape: pl.BlockSpec(shape, lambda b, j, pt: (0, 0))
    in_specs = [per_b((1, nq, hd)), per_b((1, nq, hd)), per_b((1, nq, hd)), per_b((1, tnew, LANES))]
    in_specs += [kv_spec(i) for i in range(pps)]
    in_specs += [kv_spec(i) for i in range(pps)]
    in_specs += [lf_spec(i) for i in range(pps)]
    in_specs += [const((LANES, LANES)), const((LANES, LANES)), const((fr, fr))]
    grid_spec = pltpu.PrefetchScalarGridSpec(
        num_scalar_prefetch=1,
        grid=(bsz, n_pages // pps),
        in_specs=in_specs,
        out_specs=per_b((1, nq, hd)),
        scratch_shapes=[
            pltpu.VMEM((nq, LANES), F32),
            pltpu.VMEM((nq, LANES), F32),
            pltpu.VMEM((nq, hd), F32),
            pltpu.VMEM((SUBLANES, LANES), F32),
            pltpu.VMEM((nq, LANES), F32),
            pltpu.VMEM((nq, nk), F32),
            pltpu.VMEM((SUBLANES, LANES), F32),
        ],
    )
    out = pl.pallas_call(
        functools.partial(_decode_kernel, heads=heads, tnew=tnew, pps=pps),
        out_shape=jax.ShapeDtypeStruct((bsz, nq, hd), F32),
        grid_spec=grid_spec,
        compiler_params=_cparams("parallel", "arbitrary"),
        name=name,
    )(page_table, qb.reshape(bsz, nq, hd), k_new.reshape(bsz, nq, hd), v_new.reshape(bsz, nq, hd), f_new,
      *([cache_k] * pps), *([cache_v] * pps), *([lf_flat] * pps), a_sel, a2_sel, sur)
    return out.reshape(bsz, tnew, d)


def _run(x, p, conv0, ssm0, w, tag, attend):
    bsz, seq, d = x.shape
    m = bsz * seq
    x2 = x.reshape(m, d)
    xb = x2.astype(BF16)
    d_inner = w["d_inner"]
    heads = d // ATT_HEAD_DIM

    zx = _proj(xb, w["w_zx"], name=f"in_proj_{tag}")
    dt_raw = _proj(xb, w["w_dt"], name=f"dt_proj_{tag}")
    act, new_conv = _conv_silu(zx.reshape(bsz, seq, -1), conv0, w["conv_w"], w["conv_b"], d_inner=d_inner,
                               name=f"conv_{tag}")
    g, new_ssm = _ssd(zx.reshape(bsz, seq, -1), act, dt_raw.reshape(bsz, seq, LANES), ssm0, w["dt_bias"], w["a_log"],
                      w["d_skip"], w["norm_w"], d_inner=d_inner, name=f"ssd_{tag}")
    x2, xb = _mm_ln(g.reshape(m, d_inner), w["w_out"], x2, w["ln1_g"][0], w["ln1_b"][0], name=f"out_proj_ln_{tag}")
    x2, xb = _ffn(x2, xb, w["wg"][0], w["wu"][0], w["wd"][0], w["ln2_g"][0], w["ln2_b"][0], name=f"ffn0_{tag}")
    x2, xb = _ple(x2, xb, p[0].reshape(m, -1), w["ple_g"][0], w["ple_p"][0], name=f"ple0_{tag}")

    k, kb = _proj(xb, w["w_k"], mode="dual", name=f"k_proj_{tag}")
    v, vb = _proj(xb, w["w_v"], mode="dual", name=f"v_proj_{tag}")
    logf_pad = _proj(xb, w["w_f"], mode="logsig", bias=w["b_f"], name=f"f_proj_{tag}")

    qb = _proj(xb, w["w_q"], mode="scale", scale=ATT_HEAD_DIM ** -0.5, out_dtype=BF16, name=f"q_proj_{tag}")
    o = attend(qb.reshape(bsz, seq, d), k.reshape(bsz, seq, d), v.reshape(bsz, seq, d), kb.reshape(bsz, seq, d),
               vb.reshape(bsz, seq, d), logf_pad.reshape(bsz, seq, LANES))
    x2, xb = _mm_ln(o.reshape(m, d).astype(BF16), w["w_o"], x2, w["ln1_g"][1], w["ln1_b"][1], name=f"o_proj_ln_{tag}")
    x2, xb = _ffn(x2, xb, w["wg"][1], w["wu"][1], w["wd"][1], w["ln2_g"][1], w["ln2_b"][1], name=f"ffn1_{tag}")
    y, _ = _ple(x2, xb, p[1].reshape(m, -1), w["ple_g"][1], w["ple_p"][1], name=f"ple1_{tag}")

    return (y.reshape(bsz, seq, d), new_conv[None], new_ssm[None],
            k.reshape(bsz, seq, heads, ATT_HEAD_DIM), v.reshape(bsz, seq, heads, ATT_HEAD_DIM),
            logf_pad.reshape(bsz, seq, LANES)[..., :heads])


def kernel(x_prompt, x_sample, p_prompt, p_sample, state_conv, state_ssm, cache_k, cache_v, cache_logf, page_table, a_w_in, a_conv_w, a_conv_b, a_dt_bias, a_log, a_d, a_norm_w, a_w_out, kv_w, kv_b_f, b_w_q, b_w_o, ln1_g, ln1_b, ffn_w_gate, ffn_w_up, ffn_w_down, ln2_g, ln2_b, ple_w_gate, ple_w_proj):
    assert a_w_in.shape[0] == 1 and b_w_q.shape[0] == 1, "one SSD layer followed by one attention layer"
    d = x_prompt.shape[-1]
    heads = d // ATT_HEAD_DIM
    d_att = heads * ATT_HEAD_DIM
    conv_dim = a_conv_w.shape[-1]
    d_inner = conv_dim - 2 * SSM_GROUPS * SSM_STATE
    ssm_heads = d_inner // SSM_HEAD_DIM
    zx_dim = d_inner + conv_dim

    def pad_cols(wm):
        return jnp.pad(wm, ((0, 0), (0, LANES - wm.shape[1])))

    bf = lambda t: t.astype(BF16)
    w = dict(
        d_inner=d_inner,
        w_zx=bf(a_w_in[0][:, :zx_dim]), w_dt=bf(pad_cols(a_w_in[0][:, zx_dim:])),
        conv_w=a_conv_w[0], conv_b=a_conv_b[0], dt_bias=a_dt_bias[0], a_log=a_log[0], d_skip=a_d[0],
        norm_w=a_norm_w[0], w_out=bf(a_w_out[0]),
        w_k=bf(kv_w[:, :d_att]), w_v=bf(kv_w[:, d_att:2 * d_att]), w_f=bf(pad_cols(kv_w[:, 2 * d_att:])),
        b_f=jnp.pad(kv_b_f.astype(F32), (0, LANES - heads)).reshape(1, LANES),
        w_q=bf(b_w_q[0]), w_o=bf(b_w_o[0]),
        ln1_g=ln1_g, ln1_b=ln1_b, ln2_g=ln2_g, ln2_b=ln2_b,
        wg=bf(ffn_w_gate), wu=bf(ffn_w_up), wd=bf(ffn_w_down), ple_g=bf(ple_w_gate), ple_p=bf(ple_w_proj),
    )
    assert ssm_heads * SSM_HEAD_DIM == d_inner

    def attend_prompt(qb, k, v, kb, vb, logf_pad):
        f, ft = _forget_cumsum(logf_pad, heads=heads, name="forget_cumsum")
        return _flash_prompt(qb, kb, vb, f, ft, name="flash_prompt")

    def attend_sample(qb, k, v, kb, vb, logf_pad):
        return _decode_attention(qb, k, v, logf_pad, cache_k, cache_v, cache_logf, page_table, name="decode_attn")

    bp = x_prompt.shape[0]
    conv0_p = jnp.zeros((bp, CONV_WIDTH - 1, conv_dim), F32)
    outs_p = _run(x_prompt, p_prompt, conv0_p, None, w, "p", attend_prompt)
    outs_s = _run(x_sample, p_sample, state_conv[0], state_ssm[0], w, "s", attend_sample)
    y_p, conv_p, ssm_p, k_p, v_p, f_p = outs_p
    y_s, conv_s, ssm_s, k_s, v_s, f_s = outs_s
    return (y_p, y_s, conv_p, ssm_p, k_p, v_p, f_p, conv_s, ssm_s, k_s, v_s, f_s)
```

```python
import functools

import jax
import jax.numpy as jnp
from jax import lax
from jax.experimental import pallas as pl
from jax.experimental.pallas import tpu as pltpu

F32 = jnp.float32
BF16 = jnp.bfloat16

DEPTH = 2
SSM_HEAD_DIM = 64
SSM_GROUPS = 8
SSM_STATE = 128
CONV_WIDTH = 4
SSD_CHUNK = 128
ATT_HEAD_DIM = 128
ALPHA = (2 * DEPTH) ** 0.25
LN_EPS = 1e-5
RMS_EPS = 1e-5
NEG_INF = -1e30

LANES = 128
SUBLANES = 8
BF16_ROWS = 2 * SUBLANES
V7X_VMEM_BYTES = 64 * 1024 * 1024
VMEM_LIMIT = V7X_VMEM_BYTES * 7 // 8


def _cparams(*sem):
    return pltpu.CompilerParams(dimension_semantics=sem, vmem_limit_bytes=VMEM_LIMIT)


def _dot(a, b):
    return jnp.dot(a, b, preferred_element_type=F32)


def _dot_nt(a, b):
    return lax.dot_general(a, b, (((1,), (1,)), ((), ())), preferred_element_type=F32)


def _split_bf16(x, parts):
    out = []
    r = x
    for i in range(parts):
        p = r.astype(BF16)
        out.append(p)
        if i + 1 < parts:
            r = r - p.astype(F32)
    return out


def _sel_dot(sel, x, parts):
    acc = None
    for p in _split_bf16(x, parts):
        t = _dot(sel, p)
        acc = t if acc is None else acc + t
    return acc


def _dot_sel(x, sel, parts):
    acc = None
    for p in _split_bf16(x, parts):
        t = _dot(p, sel)
        acc = t if acc is None else acc + t
    return acc


def _softplus(x):
    return jnp.maximum(x, 0.0) + jnp.log1p(jnp.exp(-jnp.abs(x)))


def _log_sigmoid(x):
    return jnp.minimum(x, 0.0) - jnp.log1p(jnp.exp(-jnp.abs(x)))


def _layer_norm(v, g, b):
    mu = jnp.mean(v, axis=-1, keepdims=True)
    d = v - mu
    var = jnp.mean(d * d, axis=-1, keepdims=True)
    return d * lax.rsqrt(var + LN_EPS) * g + b


def _tile(n, cap):
    t = min(n, cap)
    assert n % t == 0, (n, cap)
    return t


RESIDENT_WEIGHT_BYTES = V7X_VMEM_BYTES // 4


def _resident(shape, index_map):
    return pl.BlockSpec(shape, index_map, pipeline_mode=pl.Buffered(1))


def _proj_kernel(*refs, mode, scale):
    if mode == "logsig":
        x_ref, w_ref, b_ref, o_ref = refs
    elif mode == "dual":
        x_ref, w_ref, o_ref, ob_ref = refs
    else:
        x_ref, w_ref, o_ref = refs
    acc = _dot(x_ref[...], w_ref[...])
    if mode == "logsig":
        o_ref[...] = _log_sigmoid(acc + b_ref[...])
    elif mode == "dual":
        o_ref[...] = acc.reshape(o_ref.shape)
        ob_ref[...] = acc.astype(BF16)
    elif mode == "scale":
        o_ref[...] = (acc * scale).astype(o_ref.dtype)
    else:
        o_ref[...] = acc.astype(o_ref.dtype)


def _proj(xb, w, *, mode="plain", out_dtype=F32, scale=1.0, bias=None, name):
    m, k = xb.shape
    n = w.shape[1]
    tm = _tile(m, 1024)
    resident = k * n * w.dtype.itemsize <= RESIDENT_WEIGHT_BYTES
    tn = n if resident else _tile(n, 1024)
    in_specs = [pl.BlockSpec((tm, k), lambda i, j: (i, 0)),
                _resident((k, tn), lambda i, j: (0, j)) if resident else pl.BlockSpec((k, tn), lambda i, j: (0, j))]
    args = [xb, w]
    o_spec = pl.BlockSpec((tm, tn), lambda i, j: (i, j))
    if mode == "logsig":
        in_specs.append(pl.BlockSpec((1, tn), lambda i, j: (0, j)))
        args.append(bias)
    if mode == "dual":
        assert tn == n and n % ATT_HEAD_DIM == 0
        heads = n // ATT_HEAD_DIM
        out_shape = (jax.ShapeDtypeStruct((m, heads, ATT_HEAD_DIM), F32), jax.ShapeDtypeStruct((m, n), BF16))
        out_specs = (pl.BlockSpec((tm, heads, ATT_HEAD_DIM), lambda i, j: (i, 0, 0)), o_spec)
    else:
        out_shape = jax.ShapeDtypeStruct((m, n), out_dtype)
        out_specs = o_spec
    return pl.pallas_call(
        functools.partial(_proj_kernel, mode=mode, scale=scale),
        out_shape=out_shape,
        grid=(m // tm, n // tn),
        in_specs=in_specs,
        out_specs=out_specs,
        compiler_params=_cparams("parallel", "arbitrary"),
        name=name,
    )(*args)


CONV_PAD = SUBLANES
CONV_ROWS = 256


def _conv_kernel(x_ref, st_ref, w_ref, b_ref, act_ref, nc_ref, pad_ref, *, seq):
    lo = CONV_PAD - (CONV_WIDTH - 1)
    pad_ref[:, lo:CONV_PAD, :] = st_ref[...]
    pad_ref[:, CONV_PAD:CONV_PAD + seq, :] = x_ref[...]
    w = w_ref[...]
    b = b_ref[...]
    rc = min(seq, CONV_ROWS)
    for r0 in range(0, seq, rc):
        acc = b + w[0:1, :] * pad_ref[:, lo + r0:lo + r0 + rc, :]
        for k in range(1, CONV_WIDTH):
            acc = acc + w[k:k + 1, :] * pad_ref[:, lo + k + r0:lo + k + r0 + rc, :]
        act_ref[:, r0:r0 + rc, :] = acc * jax.nn.sigmoid(acc)
    nc_ref[...] = pad_ref[:, lo + seq:CONV_PAD + seq, :]


def _conv_silu(zx, conv_state, conv_w, conv_b, *, d_inner, name):
    bsz, seq, _ = zx.shape
    conv_dim = conv_w.shape[1]
    if seq >= SSD_CHUNK:
        bb, tc = 1, 512
    else:
        bb, tc = _tile(bsz, 16), 2048
    assert conv_dim % tc == 0 and d_inner % tc == 0
    off = d_inner // tc
    return pl.pallas_call(
        functools.partial(_conv_kernel, seq=seq),
        out_shape=(jax.ShapeDtypeStruct((bsz, seq, conv_dim), F32),
                   jax.ShapeDtypeStruct((bsz, CONV_WIDTH - 1, conv_dim), F32)),
        grid=(bsz // bb, conv_dim // tc),
        in_specs=[
            pl.BlockSpec((bb, seq, tc), lambda b, j: (b, 0, off + j)),
            pl.BlockSpec((bb, CONV_WIDTH - 1, tc), lambda b, j: (b, 0, j)),
            pl.BlockSpec((CONV_WIDTH, tc), lambda b, j: (0, j)),
            pl.BlockSpec((1, tc), lambda b, j: (0, j)),
        ],
        out_specs=(
            pl.BlockSpec((bb, seq, tc), lambda b, j: (b, 0, j)),
            pl.BlockSpec((bb, CONV_WIDTH - 1, tc), lambda b, j: (b, 0, j)),
        ),
        scratch_shapes=[pltpu.VMEM((bb, CONV_PAD + seq, tc), F32)],
        compiler_params=_cparams("parallel", "parallel"),
        name=name,
    )(zx, conv_state, conv_w, conv_b.reshape(1, conv_dim))


def _ssd_kernel(*refs, rows, cl, has_h0, d_inner, n_state):
    groups = SSM_GROUPS
    gw = d_inner // groups
    pairs = gw // LANES
    heads_per_group = gw // SSM_HEAD_DIM
    padded = rows < cl
    it = iter(refs)
    xs_ref, bm_ref, cm_ref, z_ref, dt_ref = next(it), next(it), next(it), next(it), next(it)
    dtb_ref, alog_ref, dsk_ref, nw_ref, tri_ref, exp_ref = (next(it), next(it), next(it), next(it), next(it), next(it))
    h0_ref = next(it) if has_h0 else None
    g_ref, hout_ref = next(it), next(it)
    ht_ref = next(it)
    if padded:
        xs_p, bm_p, cm_p, z_p, dt_p = next(it), next(it), next(it), next(it), next(it)

    c = pl.program_id(1)

    @pl.when(c == 0)
    def _init():
        if has_h0:
            for i in range(d_inner // LANES):
                ht_ref[:, i * LANES:(i + 1) * LANES] = h0_ref[0, i * LANES:(i + 1) * LANES, :].T
        else:
            ht_ref[...] = jnp.zeros_like(ht_ref)

    if padded:
        for src, dst in ((xs_ref, xs_p), (bm_ref, bm_p), (cm_ref, cm_p), (z_ref, z_p), (dt_ref, dt_p)):
            dst[...] = jnp.zeros_like(dst)
            dst[0:rows, :] = src[0]
        xs_v, bm_v, cm_v, z_v, dt_v = xs_p, bm_p, cm_p, z_p, dt_p
        rd = lambda ref, sl: ref[:, sl]
    else:
        xs_v, bm_v, cm_v, z_v, dt_v = xs_ref, bm_ref, cm_ref, z_ref, dt_ref
        rd = lambda ref, sl: ref[0, :, sl]

    full = slice(None)
    dt = _softplus(rd(dt_v, full) + dtb_ref[...])
    if padded:
        row = lax.broadcasted_iota(jnp.int32, dt.shape, 0)
        dt = jnp.where(row < rows, dt, 0.0)
    a = -jnp.exp(alog_ref[...])
    tri = tri_ref[...]
    a_cum = _sel_dot(tri, dt * a, 3)
    a_cum_t = a_cum.T
    a_last = a_cum[cl - 1:cl, :]
    ea = jnp.exp(a_cum)
    de = jnp.exp(a_last - a_cum)

    li = lax.broadcasted_iota(jnp.int32, (cl, cl), 0)
    si = lax.broadcasted_iota(jnp.int32, (cl, cl), 1)
    causal = li >= si
    lane = lax.broadcasted_iota(jnp.int32, (cl, LANES), 1)
    first_head = lane < SSM_HEAD_DIM

    for g in range(groups):
        gs = slice(g * gw, (g + 1) * gw)
        ns = slice(g * n_state, (g + 1) * n_state)
        expand = exp_ref[:, gs]
        xs_g = rd(xs_v, gs)
        dt_e = _dot_sel(dt, expand, 2)
        ea_e = _dot_sel(ea, expand, 2)
        de_e = _dot_sel(de, expand, 2)
        xdt = xs_g * dt_e
        bm_t = rd(bm_v, ns).T.astype(BF16)
        cm_b = rd(cm_v, ns).astype(BF16)
        cb = _dot(cm_b, bm_t)
        ht_g = ht_ref[:, gs]
        y_off = _dot(cm_b, ht_g.astype(BF16)) * ea_e
        y_parts = []
        for j in range(pairs):
            sc = []
            for e in range(2):
                h = g * heads_per_group + 2 * j + e
                seg = a_cum[:, h:h + 1] - a_cum_t[h:h + 1, :]
                dec = jnp.exp(jnp.where(causal, seg, -jnp.inf))
                sc.append((cb * dec).astype(BF16))
            lhs = jnp.concatenate(sc, axis=1)
            xp = xdt[:, j * LANES:(j + 1) * LANES]
            rhs = jnp.concatenate([jnp.where(first_head, xp, 0.0), jnp.where(first_head, 0.0, xp)],
                                  axis=0).astype(BF16)
            y_parts.append(_dot(lhs, rhs))
        y_g = jnp.concatenate(y_parts, axis=1) + y_off + dsk_ref[:, gs] * xs_g
        ht_ref[:, gs] = ea_e[cl - 1:cl, :] * ht_g + _dot(bm_t, (xdt * de_e).astype(BF16))
        z_g = rd(z_v, gs)
        gz = y_g * (z_g * jax.nn.sigmoid(z_g))
        ms = jnp.mean(gz * gz, axis=-1, keepdims=True)
        out_g = gz * lax.rsqrt(ms + RMS_EPS) * nw_ref[:, gs]
        g_ref[0, :, gs] = out_g[0:rows, :].astype(BF16)

    @pl.when(c == pl.num_programs(1) - 1)
    def _fin():
        for i in range(d_inner // LANES):
            hout_ref[0, i * LANES:(i + 1) * LANES, :] = ht_ref[:, i * LANES:(i + 1) * LANES].T


def _ssd(zx, act, dt_raw, ssm0, dt_bias, a_log, d_skip, norm_w, *, d_inner, name):
    bsz, seq, _ = act.shape
    n_state = SSM_STATE
    heads = d_inner // SSM_HEAD_DIM
    gn = SSM_GROUPS * n_state
    assert heads <= LANES and d_inner % gn == 0 and gn % LANES == 0
    if seq % SSD_CHUNK == 0:
        rows = cl = SSD_CHUNK
    else:
        rows, cl = seq, -(-seq // BF16_ROWS) * BF16_ROWS
    nc = seq // rows
    has_h0 = ssm0 is not None

    def pad_heads(v):
        return jnp.pad(v.astype(F32), (0, LANES - heads)).reshape(1, LANES)

    tri = (jnp.arange(cl)[:, None] >= jnp.arange(cl)[None, :]).astype(BF16)
    expand = (jnp.arange(LANES)[:, None] == (jnp.arange(d_inner) // SSM_HEAD_DIM)[None, :]).astype(BF16)
    dsk = jnp.repeat(d_skip.astype(F32), SSM_HEAD_DIM).reshape(1, d_inner)
    nw = norm_w.astype(F32).reshape(1, d_inner)
    bc_off = d_inner // gn

    const = lambda shape: pl.BlockSpec(shape, lambda b, c: (0,) * len(shape))
    in_specs = [
        pl.BlockSpec((1, rows, d_inner), lambda b, c: (b, c, 0)),
        pl.BlockSpec((1, rows, gn), lambda b, c: (b, c, bc_off)),
        pl.BlockSpec((1, rows, gn), lambda b, c: (b, c, bc_off + 1)),
        pl.BlockSpec((1, rows, d_inner), lambda b, c: (b, c, 0)),
        pl.BlockSpec((1, rows, LANES), lambda b, c: (b, c, 0)),
        const((1, LANES)), const((1, LANES)), const((1, d_inner)), const((1, d_inner)),
        const((cl, cl)), const((LANES, d_inner)),
    ]
    args = [act, act, act, zx, dt_raw, pad_heads(dt_bias), pad_heads(a_log), dsk, nw, tri, expand]
    if has_h0:
        in_specs.append(pl.BlockSpec((1, d_inner, n_state), lambda b, c: (b, 0, 0)))
        args.append(ssm0.reshape(bsz, d_inner, n_state))
    scratch = [pltpu.VMEM((n_state, d_inner), F32)]
    if rows < cl:
        scratch += [pltpu.VMEM((cl, d_inner), F32), pltpu.VMEM((cl, gn), F32), pltpu.VMEM((cl, gn), F32),
                    pltpu.VMEM((cl, d_inner), F32), pltpu.VMEM((cl, LANES), F32)]
    g, hout = pl.pallas_call(
        functools.partial(_ssd_kernel, rows=rows, cl=cl, has_h0=has_h0, d_inner=d_inner, n_state=n_state),
        out_shape=(jax.ShapeDtypeStruct((bsz, seq, d_inner), BF16),
                   jax.ShapeDtypeStruct((bsz, d_inner, n_state), F32)),
        grid=(bsz, nc),
        in_specs=in_specs,
        out_specs=(pl.BlockSpec((1, rows, d_inner), lambda b, c: (b, c, 0)),
                   pl.BlockSpec((1, d_inner, n_state), lambda b, c: (b, 0, 0))),
        scratch_shapes=scratch,
        compiler_params=_cparams("parallel", "arbitrary"),
        name=name,
    )(*args)
    return g, hout.reshape(bsz, heads, SSM_HEAD_DIM, n_state)


def _mm_ln_kernel(x_ref, w_ref, res_ref, g_ref, b_ref, o_ref, ob_ref):
    y = _layer_norm(ALPHA * res_ref[...] + _dot(x_ref[...], w_ref[...]), g_ref[...], b_ref[...])
    o_ref[...] = y
    ob_ref[...] = y.astype(BF16)


def _mm_ln(xb, w, res, gamma, beta, *, name):
    m, kdim = xb.shape
    n = w.shape[1]
    assert kdim * n * w.dtype.itemsize <= RESIDENT_WEIGHT_BYTES
    tm = _tile(m, 512)
    row = lambda shape: pl.BlockSpec(shape, lambda i: (i, 0))
    return pl.pallas_call(
        _mm_ln_kernel,
        out_shape=(jax.ShapeDtypeStruct((m, n), F32), jax.ShapeDtypeStruct((m, n), BF16)),
        grid=(m // tm,),
        in_specs=[
            row((tm, kdim)),
            _resident((kdim, n), lambda i: (0, 0)),
            row((tm, n)),
            _resident((1, n), lambda i: (0, 0)),
            _resident((1, n), lambda i: (0, 0)),
        ],
        out_specs=(row((tm, n)), row((tm, n))),
        compiler_params=_cparams("parallel"),
        name=name,
    )(xb, w, res, gamma.reshape(1, n), beta.reshape(1, n))


def _ffn_kernel(x_ref, xb_ref, wg_ref, wu_ref, wd_ref, g_ref, b_ref, o_ref, ob_ref, acc_ref, h_ref, *, nf):
    s = pl.program_id(1)

    def hidden(slot):
        xb = xb_ref[...]
        a = _dot(xb, wg_ref[...])
        u = _dot(xb, wu_ref[...])
        h_ref[slot] = (a * jax.nn.sigmoid(a) * u).astype(BF16)

    def down(slot):
        return _dot(h_ref[slot], wd_ref[...])

    @pl.when(s == 0)
    def _():
        hidden(0)

    @pl.when(s == 1)
    def _():
        hidden(1)
        acc_ref[...] = down(0)

    @pl.when(jnp.logical_and(s > 1, s < nf))
    def _():
        hidden(s % 2)
        acc_ref[...] += down((s - 1) % 2)

    @pl.when(s == nf)
    def _():
        y = _layer_norm(ALPHA * x_ref[...] + (acc_ref[...] + down((nf - 1) % 2)), g_ref[...], b_ref[...])
        o_ref[...] = y
        ob_ref[...] = y.astype(BF16)


def _ffn(x, xb, wg, wu, wd, gamma, beta, *, name):
    m, d = x.shape
    dff = wg.shape[1]
    tm = _tile(m, 512)
    tf = _tile(dff, 512)
    nf = dff // tf
    assert nf >= 2
    row = lambda shape: pl.BlockSpec(shape, lambda i, s: (i, 0))
    vec = _resident((1, d), lambda i, s: (0, 0))
    return pl.pallas_call(
        functools.partial(_ffn_kernel, nf=nf),
        out_shape=(jax.ShapeDtypeStruct((m, d), F32), jax.ShapeDtypeStruct((m, d), BF16)),
        grid=(m // tm, nf + 1),
        in_specs=[
            row((tm, d)), row((tm, d)),
            pl.BlockSpec((d, tf), lambda i, s: (0, jnp.minimum(s, nf - 1))),
            pl.BlockSpec((d, tf), lambda i, s: (0, jnp.minimum(s, nf - 1))),
            pl.BlockSpec((tf, d), lambda i, s: (jnp.maximum(s - 1, 0), 0)),
            vec, vec,
        ],
        out_specs=(row((tm, d)), row((tm, d))),
        scratch_shapes=[pltpu.VMEM((tm, d), F32), pltpu.VMEM((2, tm, tf), BF16)],
        compiler_params=_cparams("parallel", "arbitrary"),
        name=name,
    )(x, xb, wg, wu, wd, gamma.reshape(1, d), beta.reshape(1, d))


def _ple_kernel(x_ref, xb_ref, p_ref, wg_ref, wp_ref, o_ref, ob_ref):
    gate = _dot(xb_ref[...], wg_ref[...])
    proj = _dot(p_ref[...].astype(BF16), wp_ref[...])
    y = x_ref[...] + jax.nn.sigmoid(gate) * proj
    o_ref[...] = y
    ob_ref[...] = y.astype(BF16)


def _ple(x, xb, p, w_gate, w_proj, *, name):
    m, d = x.shape
    pd = p.shape[1]
    assert d * d * w_gate.dtype.itemsize <= RESIDENT_WEIGHT_BYTES
    tm = _tile(m, 512)
    row = lambda width: pl.BlockSpec((tm, width), lambda i: (i, 0))
    return pl.pallas_call(
        _ple_kernel,
        out_shape=(jax.ShapeDtypeStruct((m, d), F32), jax.ShapeDtypeStruct((m, d), BF16)),
        grid=(m // tm,),
        in_specs=[
            row(d), row(d), row(pd),
            _resident((d, d), lambda i: (0, 0)),
            _resident((pd, d), lambda i: (0, 0)),
        ],
        out_specs=(row(d), row(d)),
        compiler_params=_cparams("parallel"),
        name=name,
    )(x, xb, p, w_gate, w_proj)


def _cumsum_kernel(x_ref, tri_ref, f_ref, ft_ref, *, seq, heads):
    cl = SSD_CHUNK
    tri = tri_ref[...]
    carry = jnp.zeros((1, LANES), F32)
    for c in range(seq // cl):
        rs = slice(c * cl, (c + 1) * cl)
        fc = _sel_dot(tri, x_ref[0, rs, :], 3) + carry
        f_ref[0, rs, :] = fc
        ft_ref[0, :, rs] = fc.T[0:heads, :]
        carry = fc[cl - 1:cl, :]


def _forget_cumsum(logf_pad, *, heads, name):
    bsz, seq, _ = logf_pad.shape
    cl = SSD_CHUNK
    tri = (jnp.arange(cl)[:, None] >= jnp.arange(cl)[None, :]).astype(BF16)
    return pl.pallas_call(
        functools.partial(_cumsum_kernel, seq=seq, heads=heads),
        out_shape=(jax.ShapeDtypeStruct((bsz, seq, LANES), F32), jax.ShapeDtypeStruct((bsz, heads, seq), F32)),
        grid=(bsz,),
        in_specs=[pl.BlockSpec((1, seq, LANES), lambda b: (b, 0, 0)), pl.BlockSpec((cl, cl), lambda b: (0, 0))],
        out_specs=(pl.BlockSpec((1, seq, LANES), lambda b: (b, 0, 0)), pl.BlockSpec((1, heads, seq), lambda b: (b, 0, 0))),
        compiler_params=_cparams("parallel"),
        name=name,
    )(logf_pad, tri)


ATT_BLOCK = 512


def _flash_kernel(q_ref, k_ref, v_ref, f_ref, ft_ref, o_ref, m_ref, l_ref, acc_ref, fq_ref, *, heads):
    qi = pl.program_id(1)
    ki = pl.program_id(2)
    tq = q_ref.shape[1]
    tk = k_ref.shape[1]
    hd = ATT_HEAD_DIM
    rep = tk // LANES

    def lanes_to_tk(v):
        return jnp.concatenate([v] * rep, axis=1)

    @pl.when(ki == 0)
    def _():
        m_ref[...] = jnp.full_like(m_ref, NEG_INF)
        l_ref[...] = jnp.zeros_like(l_ref)
        acc_ref[...] = jnp.zeros_like(acc_ref)
        f_q = f_ref[0]
        for h in range(heads):
            fq_ref[h] = jnp.broadcast_to(f_q[:, h:h + 1], (tq, LANES))

    def step(diagonal):
        f_k = ft_ref[0]
        if diagonal:
            mask = lax.broadcasted_iota(jnp.int32, (tq, tk), 0) >= lax.broadcasted_iota(jnp.int32, (tq, tk), 1)
        for h in range(heads):
            hs = slice(h * hd, (h + 1) * hd)
            s = _dot_nt(q_ref[0, :, hs], k_ref[0, :, hs])
            x = s + (lanes_to_tk(fq_ref[h]) - f_k[h:h + 1, :])
            if diagonal:
                x = jnp.where(mask, x, NEG_INF)
            m_prev = m_ref[h]
            m_new = jnp.maximum(m_prev, jnp.max(x, axis=-1, keepdims=True))
            p = jnp.exp(x - lanes_to_tk(m_new))
            alpha = jnp.exp(m_prev - m_new)
            l_ref[h] = alpha * l_ref[h] + jnp.sum(p, axis=-1, keepdims=True)
            acc_ref[h] = alpha * acc_ref[h] + _dot(p.astype(BF16), v_ref[0, :, hs])
            m_ref[h] = m_new

    @pl.when(ki < qi)
    def _():
        step(False)

    @pl.when(ki == qi)
    def _():
        step(True)
        for h in range(heads):
            o_ref[0, :, h * hd:(h + 1) * hd] = (acc_ref[h] / l_ref[h]).astype(o_ref.dtype)


def _flash_prompt(qb, kb, vb, f, ft, *, name):
    bsz, seq, d = qb.shape
    heads = d // ATT_HEAD_DIM
    t = _tile(seq, ATT_BLOCK)
    nb = seq // t
    return pl.pallas_call(
        functools.partial(_flash_kernel, heads=heads),
        out_shape=jax.ShapeDtypeStruct((bsz, seq, d), BF16),
        grid=(bsz, nb, nb),
        in_specs=[
            pl.BlockSpec((1, t, d), lambda b, i, j: (b, i, 0)),
            pl.BlockSpec((1, t, d), lambda b, i, j: (b, jnp.minimum(i, j), 0)),
            pl.BlockSpec((1, t, d), lambda b, i, j: (b, jnp.minimum(i, j), 0)),
            pl.BlockSpec((1, t, LANES), lambda b, i, j: (b, i, 0)),
            pl.BlockSpec((1, heads, t), lambda b, i, j: (b, 0, jnp.minimum(i, j))),
        ],
        out_specs=pl.BlockSpec((1, t, d), lambda b, i, j: (b, i, 0)),
        scratch_shapes=[pltpu.VMEM((heads, t, LANES), F32), pltpu.VMEM((heads, t, LANES), F32),
                        pltpu.VMEM((heads, t, ATT_HEAD_DIM), F32), pltpu.VMEM((heads, t, LANES), F32)],
        compiler_params=_cparams("parallel", "parallel", "arbitrary"),
        name=name,
    )(qb, kb, vb, f, ft)


PAGES_PER_STEP = 4


def _decode_kernel(pt_ref, q_ref, kn_ref, vn_ref, fn_ref, *rest, heads, tnew, pps):
    del pt_ref
    k_refs = rest[0:pps]
    v_refs = rest[pps:2 * pps]
    lf_refs = rest[2 * pps:3 * pps]
    a_ref, a2_ref, sur_ref, o_ref = rest[3 * pps:3 * pps + 4]
    m_ref, l_ref, acc_ref, carry_ref, cnrep_ref, madd_ref, cnt_ref = rest[3 * pps + 4:]
    j = pl.program_id(1)
    hd = ATT_HEAD_DIM
    page = k_refs[0].shape[1]
    nq = tnew * heads
    nk = page * heads
    fr = nk // LANES

    def widen(v, w):
        return v[:, 0:w] if w <= LANES else jnp.concatenate([v] * (w // LANES), axis=1)

    def update(xs, vbs):
        m_prev = m_ref[...]
        m_new = m_prev
        for x in xs:
            m_new = jnp.maximum(m_new, jnp.max(x, axis=-1, keepdims=True))
        alpha = jnp.exp(m_prev - m_new)
        l_new = alpha * l_ref[...]
        acc = alpha * acc_ref[...]
        for x, vb in zip(xs, vbs):
            p = jnp.exp(x - widen(m_new, x.shape[1]))
            l_new = l_new + jnp.sum(p, axis=-1, keepdims=True)
            acc = acc + _dot(p.astype(BF16), vb)
        l_ref[...] = l_new
        acc_ref[...] = acc
        m_ref[...] = m_new

    @pl.when(j == 0)
    def _init():
        m_ref[...] = jnp.full_like(m_ref, NEG_INF)
        l_ref[...] = jnp.zeros_like(l_ref)
        acc_ref[...] = jnp.zeros_like(acc_ref)
        carry_ref[...] = jnp.zeros_like(carry_ref)
        cnt_ref[...] = jnp.zeros_like(cnt_ref)
        run = jnp.zeros((1, LANES), F32)
        for t in range(tnew):
            run = run + fn_ref[0, t:t + 1, :]
            cnt_ref[t:t + 1, :] = run
        cn = cnt_ref[...]
        cn_t = cn.T
        cncol = jnp.concatenate([cn_t[0:heads, t:t + 1] for t in range(tnew)], axis=0)
        cnrep_ref[...] = jnp.broadcast_to(cncol, (nq, LANES))
        r = lax.broadcasted_iota(jnp.int32, (nq, nk), 0)
        c = lax.broadcasted_iota(jnp.int32, (nq, nk), 1)
        madd_ref[...] = jnp.where(r % heads == c % heads, jnp.broadcast_to(cncol, (nq, nk)), NEG_INF)
        s = _dot_nt(q_ref[0], kn_ref[0].astype(BF16))
        brow = jnp.concatenate([cn[t:t + 1, 0:heads] for t in range(tnew)], axis=1)
        x = s + (cnrep_ref[:, 0:nq] - brow)
        rr = lax.broadcasted_iota(jnp.int32, (nq, nq), 0)
        cc = lax.broadcasted_iota(jnp.int32, (nq, nq), 1)
        ok = jnp.logical_and(rr % heads == cc % heads, cc // heads <= rr // heads)
        update([jnp.where(ok, x, NEG_INF)], [vn_ref[0].astype(BF16)])

    a_sel = a_ref[...]
    a2_sel = a2_ref[...]
    sur = sur_ref[...]
    q = q_ref[0]
    xs, vbs = [], []
    for i in range(pps):
        lf = lf_refs[i][0]
        rowtot = _dot_sel(lf, a2_sel, 3)
        suf = _dot_sel(lf, a_sel, 3) + _sel_dot(sur, rowtot, 3) + carry_ref[0:1, :]
        carry_ref[0:1, :] = carry_ref[0:1, :] + jnp.sum(rowtot, axis=0, keepdims=True)
        brow = jnp.concatenate([suf[rw:rw + 1, :] for rw in range(fr)], axis=1)
        k2 = k_refs[i][0].reshape(nk, hd).astype(BF16)
        xs.append(_dot_nt(q, k2) + (brow + madd_ref[...]))
        vbs.append(v_refs[i][0].reshape(nk, hd).astype(BF16))
    update(xs, vbs)

    @pl.when(j == pl.num_programs(1) - 1)
    def _fin():
        o_ref[0] = acc_ref[...] / l_ref[...]


def _decode_attention(qb, k_new, v_new, f_new, cache_k, cache_v, cache_logf, page_table, *, name):
    bsz, tnew, d = qb.shape
    n_pool, page, heads, hd = cache_k.shape
    n_pages = page_table.shape[1]
    pps = PAGES_PER_STEP
    nq = tnew * heads
    nk = page * heads
    fr = nk // LANES
    assert n_pages % pps == 0 and tnew <= SUBLANES and hd == ATT_HEAD_DIM and heads * hd == d
    assert LANES % heads == 0 and nk % LANES == 0
    lane = jnp.arange(LANES)
    same_head = (lane[:, None] % heads) == (lane[None, :] % heads)
    a2_sel = same_head.astype(BF16)
    a_sel = jnp.logical_and(same_head, lane[:, None] // heads > lane[None, :] // heads).astype(BF16)
    sur = (jnp.arange(fr)[None, :] > jnp.arange(fr)[:, None]).astype(BF16)
    lf_flat = cache_logf.reshape(n_pool, fr, LANES)

    def slot(b, j, pt, i):
        return pt[b, n_pages - 1 - (j * pps + i)]

    def kv_spec(i):
        return pl.BlockSpec((1, page, heads, hd), lambda b, j, pt: (slot(b, j, pt, i), 0, 0, 0))

    def lf_spec(i):
        return pl.BlockSpec((1, fr, LANES), lambda b, j, pt: (slot(b, j, pt, i), 0, 0))

    per_b = lambda shape: pl.BlockSpec(shape, lambda b, j, pt: (b, 0, 0))
    const = lambda shape: pl.BlockSpec(shape, lambda b, j, pt: (0, 0))
    in_specs = [per_b((1, nq, hd)), per_b((1, nq, hd)), per_b((1, nq, hd)), per_b((1, tnew, LANES))]
    in_specs += [kv_spec(i) for i in range(pps)]
    in_specs += [kv_spec(i) for i in range(pps)]
    in_specs += [lf_spec(i) for i in range(pps)]
    in_specs += [const((LANES, LANES)), const((LANES, LANES)), const((fr, fr))]
    grid_spec = pltpu.PrefetchScalarGridSpec(
        num_scalar_prefetch=1,
        grid=(bsz, n_pages // pps),
        in_specs=in_specs,
        out_specs=per_b((1, nq, hd)),
        scratch_shapes=[
            pltpu.VMEM((nq, LANES), F32),
            pltpu.VMEM((nq, LANES), F32),
            pltpu.VMEM((nq, hd), F32),
            pltpu.VMEM((SUBLANES, LANES), F32),
            pltpu.VMEM((nq, LANES), F32),
            pltpu.VMEM((nq, nk), F32),
            pltpu.VMEM((SUBLANES, LANES), F32),
        ],
    )
    out = pl.pallas_call(
        functools.partial(_decode_kernel, heads=heads, tnew=tnew, pps=pps),
        out_shape=jax.ShapeDtypeStruct((bsz, nq, hd), F32),
        grid_spec=grid_spec,
        compiler_params=_cparams("parallel", "arbitrary"),
        name=name,
    )(page_table, qb.reshape(bsz, nq, hd), k_new, v_new, f_new,
      *([cache_k] * pps), *([cache_v] * pps), *([lf_flat] * pps), a_sel, a2_sel, sur)
    return out.reshape(bsz, tnew, d)


def _run(x, p, conv0, ssm0, w, tag, attend):
    bsz, seq, d = x.shape
    m = bsz * seq
    x2 = x.reshape(m, d)
    xb = x2.astype(BF16)
    d_inner = w["d_inner"]
    heads = d // ATT_HEAD_DIM

    zx = _proj(xb, w["w_zx"], name=f"in_proj_{tag}")
    dt_raw = _proj(xb, w["w_dt"], name=f"dt_proj_{tag}")
    act, new_conv = _conv_silu(zx.reshape(bsz, seq, -1), conv0, w["conv_w"], w["conv_b"], d_inner=d_inner,
                               name=f"conv_{tag}")
    g, new_ssm = _ssd(zx.reshape(bsz, seq, -1), act, dt_raw.reshape(bsz, seq, LANES), ssm0, w["dt_bias"], w["a_log"],
                      w["d_skip"], w["norm_w"], d_inner=d_inner, name=f"ssd_{tag}")
    x2, xb = _mm_ln(g.reshape(m, d_inner), w["w_out"], x2, w["ln1_g"][0], w["ln1_b"][0], name=f"out_proj_ln_{tag}")
    x2, xb = _ffn(x2, xb, w["wg"][0], w["wu"][0], w["wd"][0], w["ln2_g"][0], w["ln2_b"][0], name=f"ffn0_{tag}")
    x2, xb = _ple(x2, xb, p[0].reshape(m, -1), w["ple_g"][0], w["ple_p"][0], name=f"ple0_{tag}")

    k, kb = _proj(xb, w["w_k"], mode="dual", name=f"k_proj_{tag}")
    v, vb = _proj(xb, w["w_v"], mode="dual", name=f"v_proj_{tag}")
    logf_pad = _proj(xb, w["w_f"], mode="logsig", bias=w["b_f"], name=f"f_proj_{tag}")

    qb = _proj(xb, w["w_q"], mode="scale", scale=ATT_HEAD_DIM ** -0.5, out_dtype=BF16, name=f"q_proj_{tag}")
    rows_heads = (bsz, seq * heads, ATT_HEAD_DIM)
    o = attend(qb.reshape(bsz, seq, d), k.reshape(rows_heads), v.reshape(rows_heads), kb.reshape(bsz, seq, d),
               vb.reshape(bsz, seq, d), logf_pad.reshape(bsz, seq, LANES))
    x2, xb = _mm_ln(o.reshape(m, d).astype(BF16), w["w_o"], x2, w["ln1_g"][1], w["ln1_b"][1], name=f"o_proj_ln_{tag}")
    x2, xb = _ffn(x2, xb, w["wg"][1], w["wu"][1], w["wd"][1], w["ln2_g"][1], w["ln2_b"][1], name=f"ffn1_{tag}")
    y, _ = _ple(x2, xb, p[1].reshape(m, -1), w["ple_g"][1], w["ple_p"][1], name=f"ple1_{tag}")

    return (y.reshape(bsz, seq, d), new_conv[None], new_ssm[None],
            k.reshape(bsz, seq, heads, ATT_HEAD_DIM), v.reshape(bsz, seq, heads, ATT_HEAD_DIM),
            logf_pad.reshape(bsz, seq, LANES)[..., :heads])


def kernel(x_prompt, x_sample, p_prompt, p_sample, state_conv, state_ssm, cache_k, cache_v, cache_logf, page_table, a_w_in, a_conv_w, a_conv_b, a_dt_bias, a_log, a_d, a_norm_w, a_w_out, kv_w, kv_b_f, b_w_q, b_w_o, ln1_g, ln1_b, ffn_w_gate, ffn_w_up, ffn_w_down, ln2_g, ln2_b, ple_w_gate, ple_w_proj):
    assert a_w_in.shape[0] == 1 and b_w_q.shape[0] == 1, "one SSD layer followed by one attention layer"
    d = x_prompt.shape[-1]
    heads = d // ATT_HEAD_DIM
    d_att = heads * ATT_HEAD_DIM
    conv_dim = a_conv_w.shape[-1]
    d_inner = conv_dim - 2 * SSM_GROUPS * SSM_STATE
    ssm_heads = d_inner // SSM_HEAD_DIM
    zx_dim = d_inner + conv_dim

    def pad_cols(wm):
        return jnp.pad(wm, ((0, 0), (0, LANES - wm.shape[1])))

    bf = lambda t: t.astype(BF16)
    w = dict(
        d_inner=d_inner,
        w_zx=bf(a_w_in[0][:, :zx_dim]), w_dt=bf(pad_cols(a_w_in[0][:, zx_dim:])),
        conv_w=a_conv_w[0], conv_b=a_conv_b[0], dt_bias=a_dt_bias[0], a_log=a_log[0], d_skip=a_d[0],
        norm_w=a_norm_w[0], w_out=bf(a_w_out[0]),
        w_k=bf(kv_w[:, :d_att]), w_v=bf(kv_w[:, d_att:2 * d_att]), w_f=bf(pad_cols(kv_w[:, 2 * d_att:])),
        b_f=jnp.pad(kv_b_f.astype(F32), (0, LANES - heads)).reshape(1, LANES),
        w_q=bf(b_w_q[0]), w_o=bf(b_w_o[0]),
        ln1_g=ln1_g, ln1_b=ln1_b, ln2_g=ln2_g, ln2_b=ln2_b,
        wg=bf(ffn_w_gate), wu=bf(ffn_w_up), wd=bf(ffn_w_down), ple_g=bf(ple_w_gate), ple_p=bf(ple_w_proj),
    )
    assert ssm_heads * SSM_HEAD_DIM == d_inner

    def attend_prompt(qb, k, v, kb, vb, logf_pad):
        f, ft = _forget_cumsum(logf_pad, heads=heads, name="forget_cumsum")
        return _flash_prompt(qb, kb, vb, f, ft, name="flash_prompt")

    def attend_sample(qb, k, v, kb, vb, logf_pad):
        return _decode_attention(qb, k, v, logf_pad, cache_k, cache_v, cache_logf, page_table, name="decode_attn")

    bp = x_prompt.shape[0]
    conv0_p = jnp.zeros((bp, CONV_WIDTH - 1, conv_dim), F32)
    outs_p = _run(x_prompt, p_prompt, conv0_p, None, w, "p", attend_prompt)
    outs_s = _run(x_sample, p_sample, state_conv[0], state_ssm[0], w, "s", attend_sample)
    y_p, conv_p, ssm_p, k_p, v_p, f_p = outs_p
    y_s, conv_s, ssm_s, k_s, v_s, f_s = outs_s
    return (y_p, y_s, conv_p, ssm_p, k_p, v_p, f_p, conv_s, ssm_s, k_s, v_s, f_s)
```

```python
import functools

import jax
import jax.numpy as jnp
from jax import lax
from jax.experimental import pallas as pl
from jax.experimental.pallas import tpu as pltpu

F32 = jnp.float32
BF16 = jnp.bfloat16

DEPTH = 2
SSM_HEAD_DIM = 64
SSM_GROUPS = 8
SSM_STATE = 128
CONV_WIDTH = 4
SSD_CHUNK = 128
ATT_HEAD_DIM = 128
ALPHA = (2 * DEPTH) ** 0.25
LN_EPS = 1e-5
RMS_EPS = 1e-5
NEG_INF = -1e30

LANES = 128
SUBLANES = 8
BF16_ROWS = 2 * SUBLANES
V7X_VMEM_BYTES = 64 * 1024 * 1024
VMEM_LIMIT = V7X_VMEM_BYTES * 7 // 8


def _cparams(*sem):
    return pltpu.CompilerParams(dimension_semantics=sem, vmem_limit_bytes=VMEM_LIMIT)


def _dot(a, b):
    return jnp.dot(a, b, preferred_element_type=F32)


def _dot_nt(a, b):
    return lax.dot_general(a, b, (((1,), (1,)), ((), ())), preferred_element_type=F32)


def _split_bf16(x, parts):
    out = []
    r = x
    for i in range(parts):
        p = r.astype(BF16)
        out.append(p)
        if i + 1 < parts:
            r = r - p.astype(F32)
    return out


def _sel_dot(sel, x, parts):
    acc = None
    for p in _split_bf16(x, parts):
        t = _dot(sel, p)
        acc = t if acc is None else acc + t
    return acc


def _dot_sel(x, sel, parts):
    acc = None
    for p in _split_bf16(x, parts):
        t = _dot(p, sel)
        acc = t if acc is None else acc + t
    return acc


def _softplus(x):
    return jnp.maximum(x, 0.0) + jnp.log1p(jnp.exp(-jnp.abs(x)))


def _log_sigmoid(x):
    return jnp.minimum(x, 0.0) - jnp.log1p(jnp.exp(-jnp.abs(x)))


def _layer_norm(v, g, b):
    mu = jnp.mean(v, axis=-1, keepdims=True)
    d = v - mu
    var = jnp.mean(d * d, axis=-1, keepdims=True)
    return d * lax.rsqrt(var + LN_EPS) * g + b


def _tile(n, cap):
    t = min(n, cap)
    assert n % t == 0, (n, cap)
    return t


RESIDENT_WEIGHT_BYTES = V7X_VMEM_BYTES // 4


def _resident(shape, index_map):
    return pl.BlockSpec(shape, index_map, pipeline_mode=pl.Buffered(1))


def _proj_kernel(*refs, mode, scale):
    if mode == "logsig":
        x_ref, w_ref, b_ref, o_ref = refs
    elif mode == "dual":
        x_ref, w_ref, o_ref, ob_ref = refs
    else:
        x_ref, w_ref, o_ref = refs
    acc = _dot(x_ref[...], w_ref[...])
    if mode == "logsig":
        o_ref[...] = _log_sigmoid(acc + b_ref[...])
    elif mode == "dual":
        o_ref[...] = acc.reshape(o_ref.shape)
        ob_ref[...] = acc.astype(BF16)
    elif mode == "scale":
        o_ref[...] = (acc * scale).astype(o_ref.dtype)
    else:
        o_ref[...] = acc.astype(o_ref.dtype)


def _proj(xb, w, *, cols=None, mode="plain", out_dtype=F32, scale=1.0, bias=None, name):
    m, k = xb.shape
    start, n = cols if cols is not None else (0, w.shape[-1])
    tm = _tile(m, 1024)
    resident = k * n * w.dtype.itemsize <= RESIDENT_WEIGHT_BYTES
    tn = n if resident else _tile(n, 1024)
    assert start % tn == 0
    j0 = start // tn
    if w.ndim == 3:
        assert w.shape[0] == 1
        w_block, w_map = (None, k, tn), lambda i, j: (0, 0, j0 + j)
    else:
        w_block, w_map = (k, tn), lambda i, j: (0, j0 + j)
    in_specs = [pl.BlockSpec((tm, k), lambda i, j: (i, 0)),
                _resident(w_block, w_map) if resident else pl.BlockSpec(w_block, w_map)]
    args = [xb, w]
    o_spec = pl.BlockSpec((tm, tn), lambda i, j: (i, j))
    if mode == "logsig":
        in_specs.append(pl.BlockSpec((1, tn), lambda i, j: (0, j)))
        args.append(bias)
    if mode == "dual":
        assert tn == n and n % ATT_HEAD_DIM == 0
        heads = n // ATT_HEAD_DIM
        out_shape = (jax.ShapeDtypeStruct((m, heads, ATT_HEAD_DIM), F32), jax.ShapeDtypeStruct((m, n), BF16))
        out_specs = (pl.BlockSpec((tm, heads, ATT_HEAD_DIM), lambda i, j: (i, 0, 0)), o_spec)
    else:
        out_shape = jax.ShapeDtypeStruct((m, n), out_dtype)
        out_specs = o_spec
    return pl.pallas_call(
        functools.partial(_proj_kernel, mode=mode, scale=scale),
        out_shape=out_shape,
        grid=(m // tm, n // tn),
        in_specs=in_specs,
        out_specs=out_specs,
        compiler_params=_cparams("parallel", "arbitrary"),
        name=name,
    )(*args)


CONV_PAD = SUBLANES
CONV_ROWS = 256


def _conv_kernel(x_ref, st_ref, w_ref, b_ref, act_ref, nc_ref, pad_ref, *, seq):
    lo = CONV_PAD - (CONV_WIDTH - 1)
    pad_ref[:, lo:CONV_PAD, :] = st_ref[...]
    pad_ref[:, CONV_PAD:CONV_PAD + seq, :] = x_ref[...]
    w = w_ref[...]
    b = b_ref[...]
    rc = min(seq, CONV_ROWS)
    for r0 in range(0, seq, rc):
        acc = b + w[0:1, :] * pad_ref[:, lo + r0:lo + r0 + rc, :]
        for k in range(1, CONV_WIDTH):
            acc = acc + w[k:k + 1, :] * pad_ref[:, lo + k + r0:lo + k + r0 + rc, :]
        act_ref[:, r0:r0 + rc, :] = acc * jax.nn.sigmoid(acc)
    nc_ref[...] = pad_ref[:, lo + seq:CONV_PAD + seq, :]


def _conv_silu(zx, conv_state, conv_w, conv_b, *, d_inner, name):
    bsz, seq, _ = zx.shape
    conv_dim = conv_w.shape[1]
    if seq >= SSD_CHUNK:
        bb, tc = 1, 512
    else:
        bb, tc = _tile(bsz, 16), 2048
    assert conv_dim % tc == 0 and d_inner % tc == 0
    off = d_inner // tc
    return pl.pallas_call(
        functools.partial(_conv_kernel, seq=seq),
        out_shape=(jax.ShapeDtypeStruct((bsz, seq, conv_dim), F32),
                   jax.ShapeDtypeStruct((bsz, CONV_WIDTH - 1, conv_dim), F32)),
        grid=(bsz // bb, conv_dim // tc),
        in_specs=[
            pl.BlockSpec((bb, seq, tc), lambda b, j: (b, 0, off + j)),
            pl.BlockSpec((bb, CONV_WIDTH - 1, tc), lambda b, j: (b, 0, j)),
            pl.BlockSpec((CONV_WIDTH, tc), lambda b, j: (0, j)),
            pl.BlockSpec((1, tc), lambda b, j: (0, j)),
        ],
        out_specs=(
            pl.BlockSpec((bb, seq, tc), lambda b, j: (b, 0, j)),
            pl.BlockSpec((bb, CONV_WIDTH - 1, tc), lambda b, j: (b, 0, j)),
        ),
        scratch_shapes=[pltpu.VMEM((bb, CONV_PAD + seq, tc), F32)],
        compiler_params=_cparams("parallel", "parallel"),
        name=name,
    )(zx, conv_state, conv_w, conv_b.reshape(1, conv_dim))


def _ssd_kernel(*refs, rows, cl, has_h0, d_inner, n_state):
    groups = SSM_GROUPS
    gw = d_inner // groups
    pairs = gw // LANES
    heads_per_group = gw // SSM_HEAD_DIM
    padded = rows < cl
    it = iter(refs)
    xs_ref, bm_ref, cm_ref, z_ref, dt_ref = next(it), next(it), next(it), next(it), next(it)
    dtb_ref, alog_ref, dsk_ref, nw_ref, tri_ref, exp_ref = (next(it), next(it), next(it), next(it), next(it), next(it))
    h0_ref = next(it) if has_h0 else None
    g_ref, hout_ref = next(it), next(it)
    ht_ref = next(it)
    if padded:
        xs_p, bm_p, cm_p, z_p, dt_p = next(it), next(it), next(it), next(it), next(it)

    c = pl.program_id(1)

    @pl.when(c == 0)
    def _init():
        if has_h0:
            for i in range(d_inner // LANES):
                ht_ref[:, i * LANES:(i + 1) * LANES] = h0_ref[0, i * LANES:(i + 1) * LANES, :].T
        else:
            ht_ref[...] = jnp.zeros_like(ht_ref)

    if padded:
        for src, dst in ((xs_ref, xs_p), (bm_ref, bm_p), (cm_ref, cm_p), (z_ref, z_p), (dt_ref, dt_p)):
            dst[...] = jnp.zeros_like(dst)
            dst[0:rows, :] = src[0]
        xs_v, bm_v, cm_v, z_v, dt_v = xs_p, bm_p, cm_p, z_p, dt_p
        rd = lambda ref, sl: ref[:, sl]
    else:
        xs_v, bm_v, cm_v, z_v, dt_v = xs_ref, bm_ref, cm_ref, z_ref, dt_ref
        rd = lambda ref, sl: ref[0, :, sl]

    full = slice(None)
    dt = _softplus(rd(dt_v, full) + dtb_ref[...])
    if padded:
        row = lax.broadcasted_iota(jnp.int32, dt.shape, 0)
        dt = jnp.where(row < rows, dt, 0.0)
    a = -jnp.exp(alog_ref[...])
    tri = tri_ref[...]
    a_cum = _sel_dot(tri, dt * a, 3)
    a_cum_t = a_cum.T
    a_last = a_cum[cl - 1:cl, :]
    ea = jnp.exp(a_cum)
    de = jnp.exp(a_last - a_cum)

    li = lax.broadcasted_iota(jnp.int32, (cl, cl), 0)
    si = lax.broadcasted_iota(jnp.int32, (cl, cl), 1)
    causal = li >= si
    lane = lax.broadcasted_iota(jnp.int32, (cl, LANES), 1)
    first_head = lane < SSM_HEAD_DIM

    for g in range(groups):
        gs = slice(g * gw, (g + 1) * gw)
        ns = slice(g * n_state, (g + 1) * n_state)
        expand = exp_ref[:, gs]
        xs_g = rd(xs_v, gs)
        dt_e = _dot_sel(dt, expand, 2)
        ea_e = _dot_sel(ea, expand, 2)
        de_e = _dot_sel(de, expand, 2)
        xdt = xs_g * dt_e
        bm_t = rd(bm_v, ns).T.astype(BF16)
        cm_b = rd(cm_v, ns).astype(BF16)
        cb = _dot(cm_b, bm_t)
        ht_g = ht_ref[:, gs]
        y_off = _dot(cm_b, ht_g.astype(BF16)) * ea_e
        y_parts = []
        for j in range(pairs):
            sc = []
            for e in range(2):
                h = g * heads_per_group + 2 * j + e
                seg = a_cum[:, h:h + 1] - a_cum_t[h:h + 1, :]
                dec = jnp.exp(jnp.where(causal, seg, -jnp.inf))
                sc.append((cb * dec).astype(BF16))
            lhs = jnp.concatenate(sc, axis=1)
            xp = xdt[:, j * LANES:(j + 1) * LANES]
            rhs = jnp.concatenate([jnp.where(first_head, xp, 0.0), jnp.where(first_head, 0.0, xp)],
                                  axis=0).astype(BF16)
            y_parts.append(_dot(lhs, rhs))
        y_g = jnp.concatenate(y_parts, axis=1) + y_off + dsk_ref[:, gs] * xs_g
        ht_ref[:, gs] = ea_e[cl - 1:cl, :] * ht_g + _dot(bm_t, (xdt * de_e).astype(BF16))
        z_g = rd(z_v, gs)
        gz = y_g * (z_g * jax.nn.sigmoid(z_g))
        ms = jnp.mean(gz * gz, axis=-1, keepdims=True)
        out_g = gz * lax.rsqrt(ms + RMS_EPS) * nw_ref[:, gs]
        g_ref[0, :, gs] = out_g[0:rows, :].astype(BF16)

    @pl.when(c == pl.num_programs(1) - 1)
    def _fin():
        for i in range(d_inner // LANES):
            hout_ref[0, i * LANES:(i + 1) * LANES, :] = ht_ref[:, i * LANES:(i + 1) * LANES].T


def _ssd(zx, act, dt_raw, ssm0, dt_bias, a_log, d_skip, norm_w, *, d_inner, name):
    bsz, seq, _ = act.shape
    n_state = SSM_STATE
    heads = d_inner // SSM_HEAD_DIM
    gn = SSM_GROUPS * n_state
    assert heads <= LANES and d_inner % gn == 0 and gn % LANES == 0
    if seq % SSD_CHUNK == 0:
        rows = cl = SSD_CHUNK
    else:
        rows, cl = seq, -(-seq // BF16_ROWS) * BF16_ROWS
    nc = seq // rows
    has_h0 = ssm0 is not None

    def pad_heads(v):
        return jnp.pad(v.astype(F32), (0, LANES - heads)).reshape(1, LANES)

    tri = (jnp.arange(cl)[:, None] >= jnp.arange(cl)[None, :]).astype(BF16)
    expand = (jnp.arange(LANES)[:, None] == (jnp.arange(d_inner) // SSM_HEAD_DIM)[None, :]).astype(BF16)
    dsk = jnp.repeat(d_skip.astype(F32), SSM_HEAD_DIM).reshape(1, d_inner)
    nw = norm_w.astype(F32).reshape(1, d_inner)
    bc_off = d_inner // gn

    const = lambda shape: pl.BlockSpec(shape, lambda b, c: (0,) * len(shape))
    in_specs = [
        pl.BlockSpec((1, rows, d_inner), lambda b, c: (b, c, 0)),
        pl.BlockSpec((1, rows, gn), lambda b, c: (b, c, bc_off)),
        pl.BlockSpec((1, rows, gn), lambda b, c: (b, c, bc_off + 1)),
        pl.BlockSpec((1, rows, d_inner), lambda b, c: (b, c, 0)),
        pl.BlockSpec((1, rows, LANES), lambda b, c: (b, c, 0)),
        const((1, LANES)), const((1, LANES)), const((1, d_inner)), const((1, d_inner)),
        const((cl, cl)), const((LANES, d_inner)),
    ]
    args = [act, act, act, zx, dt_raw, pad_heads(dt_bias), pad_heads(a_log), dsk, nw, tri, expand]
    if has_h0:
        in_specs.append(pl.BlockSpec((1, d_inner, n_state), lambda b, c: (b, 0, 0)))
        args.append(ssm0.reshape(bsz, d_inner, n_state))
    scratch = [pltpu.VMEM((n_state, d_inner), F32)]
    if rows < cl:
        scratch += [pltpu.VMEM((cl, d_inner), F32), pltpu.VMEM((cl, gn), F32), pltpu.VMEM((cl, gn), F32),
                    pltpu.VMEM((cl, d_inner), F32), pltpu.VMEM((cl, LANES), F32)]
    g, hout = pl.pallas_call(
        functools.partial(_ssd_kernel, rows=rows, cl=cl, has_h0=has_h0, d_inner=d_inner, n_state=n_state),
        out_shape=(jax.ShapeDtypeStruct((bsz, seq, d_inner), BF16),
                   jax.ShapeDtypeStruct((bsz, d_inner, n_state), F32)),
        grid=(bsz, nc),
        in_specs=in_specs,
        out_specs=(pl.BlockSpec((1, rows, d_inner), lambda b, c: (b, c, 0)),
                   pl.BlockSpec((1, d_inner, n_state), lambda b, c: (b, 0, 0))),
        scratch_shapes=scratch,
        compiler_params=_cparams("parallel", "arbitrary"),
        name=name,
    )(*args)
    return g, hout.reshape(bsz, heads, SSM_HEAD_DIM, n_state)


def _mm_ln_kernel(x_ref, w_ref, res_ref, g_ref, b_ref, o_ref, ob_ref):
    y = _layer_norm(ALPHA * res_ref[...] + _dot(x_ref[...], w_ref[...]), g_ref[...], b_ref[...])
    o_ref[...] = y
    ob_ref[...] = y.astype(BF16)


def _mm_ln(xb, w, res, gamma, beta, layer, *, name):
    m, kdim = xb.shape
    n = w.shape[-1]
    assert w.shape[0] == 1 and kdim * n * w.dtype.itemsize <= RESIDENT_WEIGHT_BYTES
    tm = _tile(m, 512)
    row = lambda shape: pl.BlockSpec(shape, lambda i: (i, 0))
    vec = _resident((None, 1, n), lambda i: (layer, 0, 0))
    return pl.pallas_call(
        _mm_ln_kernel,
        out_shape=(jax.ShapeDtypeStruct((m, n), F32), jax.ShapeDtypeStruct((m, n), BF16)),
        grid=(m // tm,),
        in_specs=[
            row((tm, kdim)),
            _resident((None, kdim, n), lambda i: (0, 0, 0)),
            row((tm, n)),
            vec, vec,
        ],
        out_specs=(row((tm, n)), row((tm, n))),
        compiler_params=_cparams("parallel"),
        name=name,
    )(xb, w, res, gamma, beta)


def _ffn_kernel(x_ref, xb_ref, wg_ref, wu_ref, wd_ref, g_ref, b_ref, o_ref, ob_ref, acc_ref):
    f = pl.program_id(1)

    @pl.when(f == 0)
    def _():
        acc_ref[...] = jnp.zeros_like(acc_ref)

    xb = xb_ref[...]
    a = _dot(xb, wg_ref[...])
    u = _dot(xb, wu_ref[...])
    h = (a * jax.nn.sigmoid(a) * u).astype(BF16)
    acc_ref[...] += _dot(h, wd_ref[...])

    @pl.when(f == pl.num_programs(1) - 1)
    def _():
        y = _layer_norm(ALPHA * x_ref[...] + acc_ref[...], g_ref[...], b_ref[...])
        o_ref[...] = y
        ob_ref[...] = y.astype(BF16)


def _ffn(x, xb, wg, wu, wd, gamma, beta, layer, *, name):
    m, d = x.shape
    dff = wg.shape[-1]
    tm = _tile(m, 512)
    tf = _tile(dff, 512)
    row = lambda shape: pl.BlockSpec(shape, lambda i, f: (i, 0))
    vec = _resident((None, 1, d), lambda i, f: (layer, 0, 0))
    return pl.pallas_call(
        _ffn_kernel,
        out_shape=(jax.ShapeDtypeStruct((m, d), F32), jax.ShapeDtypeStruct((m, d), BF16)),
        grid=(m // tm, dff // tf),
        in_specs=[
            row((tm, d)), row((tm, d)),
            pl.BlockSpec((None, d, tf), lambda i, f: (layer, 0, f)),
            pl.BlockSpec((None, d, tf), lambda i, f: (layer, 0, f)),
            pl.BlockSpec((None, tf, d), lambda i, f: (layer, f, 0)),
            vec, vec,
        ],
        out_specs=(row((tm, d)), row((tm, d))),
        scratch_shapes=[pltpu.VMEM((tm, d), F32)],
        compiler_params=_cparams("parallel", "arbitrary"),
        name=name,
    )(x, xb, wg, wu, wd, gamma, beta)


def _ple_kernel(x_ref, xb_ref, p_ref, wg_ref, wp_ref, o_ref, ob_ref):
    gate = _dot(xb_ref[...], wg_ref[...])
    proj = _dot(p_ref[...].astype(BF16), wp_ref[...])
    y = x_ref[...] + jax.nn.sigmoid(gate) * proj
    o_ref[...] = y
    ob_ref[...] = y.astype(BF16)


def _ple(x, xb, p, w_gate, w_proj, layer, *, name):
    m, d = x.shape
    pd = p.shape[-1]
    assert d * d * w_gate.dtype.itemsize <= RESIDENT_WEIGHT_BYTES
    tm = _tile(m, 512)
    row = lambda width: pl.BlockSpec((tm, width), lambda i: (i, 0))
    return pl.pallas_call(
        _ple_kernel,
        out_shape=(jax.ShapeDtypeStruct((m, d), F32), jax.ShapeDtypeStruct((m, d), BF16)),
        grid=(m // tm,),
        in_specs=[
            row(d), row(d),
            pl.BlockSpec((None, tm, pd), lambda i: (layer, i, 0)),
            _resident((None, d, d), lambda i: (layer, 0, 0)),
            _resident((None, pd, d), lambda i: (layer, 0, 0)),
        ],
        out_specs=(row(d), row(d)),
        compiler_params=_cparams("parallel"),
        name=name,
    )(x, xb, p, w_gate, w_proj)


def _cumsum_kernel(x_ref, tri_ref, f_ref, ft_ref, *, seq, heads):
    cl = SSD_CHUNK
    tri = tri_ref[...]
    carry = jnp.zeros((1, LANES), F32)
    for c in range(seq // cl):
        rs = slice(c * cl, (c + 1) * cl)
        fc = _sel_dot(tri, x_ref[0, rs, :], 3) + carry
        f_ref[0, rs, :] = fc
        ft_ref[0, :, rs] = fc.T[0:heads, :]
        carry = fc[cl - 1:cl, :]


def _forget_cumsum(logf_pad, *, heads, name):
    bsz, seq, _ = logf_pad.shape
    cl = SSD_CHUNK
    tri = (jnp.arange(cl)[:, None] >= jnp.arange(cl)[None, :]).astype(BF16)
    return pl.pallas_call(
        functools.partial(_cumsum_kernel, seq=seq, heads=heads),
        out_shape=(jax.ShapeDtypeStruct((bsz, seq, LANES), F32), jax.ShapeDtypeStruct((bsz, heads, seq), F32)),
        grid=(bsz,),
        in_specs=[pl.BlockSpec((1, seq, LANES), lambda b: (b, 0, 0)), pl.BlockSpec((cl, cl), lambda b: (0, 0))],
        out_specs=(pl.BlockSpec((1, seq, LANES), lambda b: (b, 0, 0)), pl.BlockSpec((1, heads, seq), lambda b: (b, 0, 0))),
        compiler_params=_cparams("parallel"),
        name=name,
    )(logf_pad, tri)


ATT_BLOCK = 512


def _flash_kernel(q_ref, k_ref, v_ref, f_ref, ft_ref, o_ref, m_ref, l_ref, acc_ref, fq_ref, *, heads):
    qi = pl.program_id(1)
    ki = pl.program_id(2)
    tq = q_ref.shape[1]
    tk = k_ref.shape[1]
    hd = ATT_HEAD_DIM
    rep = tk // LANES

    def lanes_to_tk(v):
        return jnp.concatenate([v] * rep, axis=1)

    @pl.when(ki == 0)
    def _():
        m_ref[...] = jnp.full_like(m_ref, NEG_INF)
        l_ref[...] = jnp.zeros_like(l_ref)
        acc_ref[...] = jnp.zeros_like(acc_ref)
        f_q = f_ref[0]
        for h in range(heads):
            fq_ref[h] = jnp.broadcast_to(f_q[:, h:h + 1], (tq, LANES))

    def step(diagonal):
        f_k = ft_ref[0]
        if diagonal:
            mask = lax.broadcasted_iota(jnp.int32, (tq, tk), 0) >= lax.broadcasted_iota(jnp.int32, (tq, tk), 1)
        for h in range(heads):
            hs = slice(h * hd, (h + 1) * hd)
            s = _dot_nt(q_ref[0, :, hs], k_ref[0, :, hs])
            x = s + (lanes_to_tk(fq_ref[h]) - f_k[h:h + 1, :])
            if diagonal:
                x = jnp.where(mask, x, NEG_INF)
            m_prev = m_ref[h]
            m_new = jnp.maximum(m_prev, jnp.max(x, axis=-1, keepdims=True))
            p = jnp.exp(x - lanes_to_tk(m_new))
            alpha = jnp.exp(m_prev - m_new)
            l_ref[h] = alpha * l_ref[h] + jnp.sum(p, axis=-1, keepdims=True)
            acc_ref[h] = alpha * acc_ref[h] + _dot(p.astype(BF16), v_ref[0, :, hs])
            m_ref[h] = m_new

    @pl.when(ki < qi)
    def _():
        step(False)

    @pl.when(ki == qi)
    def _():
        step(True)
        for h in range(heads):
            o_ref[0, :, h * hd:(h + 1) * hd] = (acc_ref[h] / l_ref[h]).astype(o_ref.dtype)


def _flash_prompt(qb, kb, vb, f, ft, *, name):
    bsz, seq, d = qb.shape
    heads = d // ATT_HEAD_DIM
    t = _tile(seq, ATT_BLOCK)
    nb = seq // t
    return pl.pallas_call(
        functools.partial(_flash_kernel, heads=heads),
        out_shape=jax.ShapeDtypeStruct((bsz, seq, d), BF16),
        grid=(bsz, nb, nb),
        in_specs=[
            pl.BlockSpec((1, t, d), lambda b, i, j: (b, i, 0)),
            pl.BlockSpec((1, t, d), lambda b, i, j: (b, jnp.minimum(i, j), 0)),
            pl.BlockSpec((1, t, d), lambda b, i, j: (b, jnp.minimum(i, j), 0)),
            pl.BlockSpec((1, t, LANES), lambda b, i, j: (b, i, 0)),
            pl.BlockSpec((1, heads, t), lambda b, i, j: (b, 0, jnp.minimum(i, j))),
        ],
        out_specs=pl.BlockSpec((1, t, d), lambda b, i, j: (b, i, 0)),
        scratch_shapes=[pltpu.VMEM((heads, t, LANES), F32), pltpu.VMEM((heads, t, LANES), F32),
                        pltpu.VMEM((heads, t, ATT_HEAD_DIM), F32), pltpu.VMEM((heads, t, LANES), F32)],
        compiler_params=_cparams("parallel", "parallel", "arbitrary"),
        name=name,
    )(qb, kb, vb, f, ft)


PAGES_PER_STEP = 8


def _decode_kernel(pt_ref, q_ref, kn_ref, vn_ref, fn_ref, *rest, heads, tnew, pps):
    del pt_ref
    k_refs = rest[0:pps]
    v_refs = rest[pps:2 * pps]
    lf_refs = rest[2 * pps:3 * pps]
    a_ref, a2_ref, sur_ref, o_ref = rest[3 * pps:3 * pps + 4]
    m_ref, l_ref, acc_ref, carry_ref, cnrep_ref, madd_ref, cnt_ref = rest[3 * pps + 4:]
    j = pl.program_id(1)
    hd = ATT_HEAD_DIM
    page = k_refs[0].shape[1]
    nq = tnew * heads
    nk = page * heads
    fr = nk // LANES

    def widen(v, w):
        return v[:, 0:w] if w <= LANES else jnp.concatenate([v] * (w // LANES), axis=1)

    def update(xs, vbs):
        m_prev = m_ref[...]
        m_new = m_prev
        for x in xs:
            m_new = jnp.maximum(m_new, jnp.max(x, axis=-1, keepdims=True))
        alpha = jnp.exp(m_prev - m_new)
        l_new = alpha * l_ref[...]
        acc = alpha * acc_ref[...]
        for x, vb in zip(xs, vbs):
            p = jnp.exp(x - widen(m_new, x.shape[1]))
            l_new = l_new + jnp.sum(p, axis=-1, keepdims=True)
            acc = acc + _dot(p.astype(BF16), vb)
        l_ref[...] = l_new
        acc_ref[...] = acc
        m_ref[...] = m_new

    @pl.when(j == 0)
    def _init():
        m_ref[...] = jnp.full_like(m_ref, NEG_INF)
        l_ref[...] = jnp.zeros_like(l_ref)
        acc_ref[...] = jnp.zeros_like(acc_ref)
        carry_ref[...] = jnp.zeros_like(carry_ref)
        cnt_ref[...] = jnp.zeros_like(cnt_ref)
        run = jnp.zeros((1, LANES), F32)
        for t in range(tnew):
            run = run + fn_ref[0, t:t + 1, :]
            cnt_ref[t:t + 1, :] = run
        cn = cnt_ref[...]
        cn_t = cn.T
        cncol = jnp.concatenate([cn_t[0:heads, t:t + 1] for t in range(tnew)], axis=0)
        cnrep_ref[...] = jnp.broadcast_to(cncol, (nq, LANES))
        r = lax.broadcasted_iota(jnp.int32, (nq, nk), 0)
        c = lax.broadcasted_iota(jnp.int32, (nq, nk), 1)
        madd_ref[...] = jnp.where(r % heads == c % heads, jnp.broadcast_to(cncol, (nq, nk)), NEG_INF)
        s = _dot_nt(q_ref[0], kn_ref[0].astype(BF16))
        brow = jnp.concatenate([cn[t:t + 1, 0:heads] for t in range(tnew)], axis=1)
        x = s + (cnrep_ref[:, 0:nq] - brow)
        rr = lax.broadcasted_iota(jnp.int32, (nq, nq), 0)
        cc = lax.broadcasted_iota(jnp.int32, (nq, nq), 1)
        ok = jnp.logical_and(rr % heads == cc % heads, cc // heads <= rr // heads)
        update([jnp.where(ok, x, NEG_INF)], [vn_ref[0].astype(BF16)])

    a_sel = a_ref[...]
    a2_sel = a2_ref[...]
    sur = sur_ref[...]
    q = q_ref[0]
    xs, vbs = [], []
    for i in range(pps):
        lf = lf_refs[i][0]
        rowtot = _dot_sel(lf, a2_sel, 3)
        suf = _dot_sel(lf, a_sel, 3) + _sel_dot(sur, rowtot, 3) + carry_ref[0:1, :]
        carry_ref[0:1, :] = carry_ref[0:1, :] + jnp.sum(rowtot, axis=0, keepdims=True)
        brow = jnp.concatenate([suf[rw:rw + 1, :] for rw in range(fr)], axis=1)
        k2 = k_refs[i][0].reshape(nk, hd).astype(BF16)
        xs.append(_dot_nt(q, k2) + (brow + madd_ref[...]))
        vbs.append(v_refs[i][0].reshape(nk, hd).astype(BF16))
    update(xs, vbs)

    @pl.when(j == pl.num_programs(1) - 1)
    def _fin():
        o_ref[0] = acc_ref[...] / l_ref[...]


def _decode_attention(qb, k_new, v_new, f_new, cache_k, cache_v, cache_logf, page_table, *, name):
    bsz, tnew, d = qb.shape
    n_pool, page, heads, hd = cache_k.shape
    n_pages = page_table.shape[1]
    pps = PAGES_PER_STEP
    nq = tnew * heads
    nk = page * heads
    fr = nk // LANES
    assert n_pages % pps == 0 and tnew <= SUBLANES and hd == ATT_HEAD_DIM and heads * hd == d
    assert LANES % heads == 0 and nk % LANES == 0
    lane = jnp.arange(LANES)
    same_head = (lane[:, None] % heads) == (lane[None, :] % heads)
    a2_sel = same_head.astype(BF16)
    a_sel = jnp.logical_and(same_head, lane[:, None] // heads > lane[None, :] // heads).astype(BF16)
    sur = (jnp.arange(fr)[None, :] > jnp.arange(fr)[:, None]).astype(BF16)
    lf_flat = cache_logf.reshape(n_pool, fr, LANES)

    def slot(b, j, pt, i):
        return pt[b, n_pages - 1 - (j * pps + i)]

    def kv_spec(i):
        return pl.BlockSpec((1, page, heads, hd), lambda b, j, pt: (slot(b, j, pt, i), 0, 0, 0))

    def lf_spec(i):
        return pl.BlockSpec((1, fr, LANES), lambda b, j, pt: (slot(b, j, pt, i), 0, 0))

    per_b = lambda shape: pl.BlockSpec(shape, lambda b, j, pt: (b, 0, 0))
    const = lambda shape: pl.BlockSpec(shape, lambda b, j, pt: (0, 0))
    in_specs = [per_b((1, nq, hd)), per_b((1, nq, hd)), per_b((1, nq, hd)), per_b((1, tnew, LANES))]
    in_specs += [kv_spec(i) for i in range(pps)]
    in_specs += [kv_spec(i) for i in range(pps)]
    in_specs += [lf_spec(i) for i in range(pps)]
    in_specs += [const((LANES, LANES)), const((LANES, LANES)), const((fr, fr))]
    grid_spec = pltpu.PrefetchScalarGridSpec(
        num_scalar_prefetch=1,
        grid=(bsz, n_pages // pps),
        in_specs=in_specs,
        out_specs=per_b((1, nq, hd)),
        scratch_shapes=[
            pltpu.VMEM((nq, LANES), F32),
            pltpu.VMEM((nq, LANES), F32),
            pltpu.VMEM((nq, hd), F32),
            pltpu.VMEM((SUBLANES, LANES), F32),
            pltpu.VMEM((nq, LANES), F32),
            pltpu.VMEM((nq, nk), F32),
            pltpu.VMEM((SUBLANES, LANES), F32),
        ],
    )
    out = pl.pallas_call(
        functools.partial(_decode_kernel, heads=heads, tnew=tnew, pps=pps),
        out_shape=jax.ShapeDtypeStruct((bsz, nq, hd), F32),
        grid_spec=grid_spec,
        compiler_params=_cparams("parallel", "arbitrary"),
        name=name,
    )(page_table, qb.reshape(bsz, nq, hd), k_new, v_new, f_new,
      *([cache_k] * pps), *([cache_v] * pps), *([lf_flat] * pps), a_sel, a2_sel, sur)
    return out.reshape(bsz, tnew, d)


def _run(x, p, conv0, ssm0, w, tag, attend):
    bsz, seq, d = x.shape
    m = bsz * seq
    x2 = x.reshape(m, d)
    xb = x2.astype(BF16)
    d_inner = w["d_inner"]
    heads = d // ATT_HEAD_DIM

    p = p.reshape(p.shape[0], m, p.shape[-1])
    ffn = lambda x2, xb, layer: _ffn(x2, xb, w["wg"], w["wu"], w["wd"], w["ln2_g"], w["ln2_b"], layer,
                                     name=f"ffn{layer}_{tag}")
    ple = lambda x2, xb, layer: _ple(x2, xb, p, w["ple_g"], w["ple_p"], layer, name=f"ple{layer}_{tag}")

    zx = _proj(xb, w["w_in"], cols=(0, w["zx_dim"]), name=f"in_proj_{tag}")
    dt_raw = _proj(xb, w["w_dt"], name=f"dt_proj_{tag}")
    act, new_conv = _conv_silu(zx.reshape(bsz, seq, -1), conv0, w["conv_w"], w["conv_b"], d_inner=d_inner,
                               name=f"conv_{tag}")
    g, new_ssm = _ssd(zx.reshape(bsz, seq, -1), act, dt_raw.reshape(bsz, seq, LANES), ssm0, w["dt_bias"], w["a_log"],
                      w["d_skip"], w["norm_w"], d_inner=d_inner, name=f"ssd_{tag}")
    x2, xb = _mm_ln(g.reshape(m, d_inner), w["w_out"], x2, w["ln1_g"], w["ln1_b"], 0, name=f"out_proj_ln_{tag}")
    x2, xb = ffn(x2, xb, 0)
    x2, xb = ple(x2, xb, 0)

    k, kb = _proj(xb, w["w_kv"], cols=(0, d), mode="dual", name=f"k_proj_{tag}")
    v, vb = _proj(xb, w["w_kv"], cols=(d, d), mode="dual", name=f"v_proj_{tag}")
    logf_pad = _proj(xb, w["w_f"], mode="logsig", bias=w["b_f"], name=f"f_proj_{tag}")

    qb = _proj(xb, w["w_q"], mode="scale", scale=ATT_HEAD_DIM ** -0.5, out_dtype=BF16, name=f"q_proj_{tag}")
    rows_heads = (bsz, seq * heads, ATT_HEAD_DIM)
    o = attend(qb.reshape(bsz, seq, d), k.reshape(rows_heads), v.reshape(rows_heads), kb.reshape(bsz, seq, d),
               vb.reshape(bsz, seq, d), logf_pad.reshape(bsz, seq, LANES))
    x2, xb = _mm_ln(o.reshape(m, d).astype(BF16), w["w_o"], x2, w["ln1_g"], w["ln1_b"], 1, name=f"o_proj_ln_{tag}")
    x2, xb = ffn(x2, xb, 1)
    y, _ = ple(x2, xb, 1)

    return (y.reshape(bsz, seq, d), new_conv[None], new_ssm[None],
            k.reshape(bsz, seq, heads, ATT_HEAD_DIM), v.reshape(bsz, seq, heads, ATT_HEAD_DIM),
            logf_pad.reshape(bsz, seq, LANES)[..., :heads])


def kernel(x_prompt, x_sample, p_prompt, p_sample, state_conv, state_ssm, cache_k, cache_v, cache_logf, page_table, a_w_in, a_conv_w, a_conv_b, a_dt_bias, a_log, a_d, a_norm_w, a_w_out, kv_w, kv_b_f, b_w_q, b_w_o, ln1_g, ln1_b, ffn_w_gate, ffn_w_up, ffn_w_down, ln2_g, ln2_b, ple_w_gate, ple_w_proj):
    assert a_w_in.shape[0] == 1 and b_w_q.shape[0] == 1, "one SSD layer followed by one attention layer"
    d = x_prompt.shape[-1]
    heads = d // ATT_HEAD_DIM
    d_att = heads * ATT_HEAD_DIM
    conv_dim = a_conv_w.shape[-1]
    d_inner = conv_dim - 2 * SSM_GROUPS * SSM_STATE
    ssm_heads = d_inner // SSM_HEAD_DIM
    zx_dim = d_inner + conv_dim

    def pad_cols(wm):
        return jnp.pad(wm, ((0, 0), (0, LANES - wm.shape[1])))

    bf = lambda t: t.astype(BF16)
    per_layer_vec = lambda t: t.astype(F32).reshape(t.shape[0], 1, t.shape[1])
    w = dict(
        d_inner=d_inner, zx_dim=zx_dim,
        w_in=bf(a_w_in), w_dt=bf(pad_cols(a_w_in[0][:, zx_dim:])),
        conv_w=a_conv_w[0], conv_b=a_conv_b[0], dt_bias=a_dt_bias[0], a_log=a_log[0], d_skip=a_d[0],
        norm_w=a_norm_w[0], w_out=bf(a_w_out),
        w_kv=bf(kv_w), w_f=bf(pad_cols(kv_w[:, 2 * d_att:])),
        b_f=jnp.pad(kv_b_f.astype(F32), (0, LANES - heads)).reshape(1, LANES),
        w_q=bf(b_w_q), w_o=bf(b_w_o),
        ln1_g=per_layer_vec(ln1_g), ln1_b=per_layer_vec(ln1_b), ln2_g=per_layer_vec(ln2_g), ln2_b=per_layer_vec(ln2_b),
        wg=bf(ffn_w_gate), wu=bf(ffn_w_up), wd=bf(ffn_w_down), ple_g=bf(ple_w_gate), ple_p=bf(ple_w_proj),
    )
    assert ssm_heads * SSM_HEAD_DIM == d_inner

    def attend_prompt(qb, k, v, kb, vb, logf_pad):
        f, ft = _forget_cumsum(logf_pad, heads=heads, name="forget_cumsum")
        return _flash_prompt(qb, kb, vb, f, ft, name="flash_prompt")

    def attend_sample(qb, k, v, kb, vb, logf_pad):
        return _decode_attention(qb, k, v, logf_pad, cache_k, cache_v, cache_logf, page_table, name="decode_attn")

    bp = x_prompt.shape[0]
    conv0_p = jnp.zeros((bp, CONV_WIDTH - 1, conv_dim), F32)
    outs_p = _run(x_prompt, p_prompt, conv0_p, None, w, "p", attend_prompt)
    outs_s = _run(x_sample, p_sample, state_conv[0], state_ssm[0], w, "s", attend_sample)
    y_p, conv_p, ssm_p, k_p, v_p, f_p = outs_p
    y_s, conv_s, ssm_s, k_s, v_s, f_s = outs_s
    return (y_p, y_s, conv_p, ssm_p, k_p, v_p, f_p, conv_s, ssm_s, k_s, v_s, f_s)
```

```python
import functools

import jax
import jax.numpy as jnp
from jax import lax
from jax.experimental import pallas as pl
from jax.experimental.pallas import tpu as pltpu

F32 = jnp.float32
BF16 = jnp.bfloat16

DEPTH = 2
SSM_HEAD_DIM = 64
SSM_GROUPS = 8
SSM_STATE = 128
CONV_WIDTH = 4
SSD_CHUNK = 128
ATT_HEAD_DIM = 128
ALPHA = (2 * DEPTH) ** 0.25
LN_EPS = 1e-5
RMS_EPS = 1e-5
NEG_INF = -1e30

LANES = 128
SUBLANES = 8
BF16_ROWS = 2 * SUBLANES
V7X_VMEM_BYTES = 64 * 1024 * 1024
VMEM_LIMIT = V7X_VMEM_BYTES * 7 // 8


def _cparams(*sem):
    return pltpu.CompilerParams(dimension_semantics=sem, vmem_limit_bytes=VMEM_LIMIT)


def _dot(a, b):
    return jnp.dot(a, b, preferred_element_type=F32)


def _dot_nt(a, b):
    return lax.dot_general(a, b, (((1,), (1,)), ((), ())), preferred_element_type=F32)


def _split_bf16(x, parts):
    out = []
    r = x
    for i in range(parts):
        p = r.astype(BF16)
        out.append(p)
        if i + 1 < parts:
            r = r - p.astype(F32)
    return out


def _sel_dot(sel, x, parts):
    acc = None
    for p in _split_bf16(x, parts):
        t = _dot(sel, p)
        acc = t if acc is None else acc + t
    return acc


def _dot_sel(x, sel, parts):
    acc = None
    for p in _split_bf16(x, parts):
        t = _dot(p, sel)
        acc = t if acc is None else acc + t
    return acc


def _softplus(x):
    return jnp.maximum(x, 0.0) + jnp.log1p(jnp.exp(-jnp.abs(x)))


def _log_sigmoid(x):
    return jnp.minimum(x, 0.0) - jnp.log1p(jnp.exp(-jnp.abs(x)))


def _layer_norm(v, g, b):
    mu = jnp.mean(v, axis=-1, keepdims=True)
    d = v - mu
    var = jnp.mean(d * d, axis=-1, keepdims=True)
    return d * lax.rsqrt(var + LN_EPS) * g + b


def _tile(n, cap):
    t = min(n, cap)
    assert n % t == 0, (n, cap)
    return t


RESIDENT_WEIGHT_BYTES = V7X_VMEM_BYTES // 4


def _resident(shape, index_map):
    return pl.BlockSpec(shape, index_map, pipeline_mode=pl.Buffered(1))


def _proj_kernel(*refs, mode, scale):
    if mode == "logsig":
        x_ref, w_ref, b_ref, o_ref = refs
    elif mode == "dual":
        x_ref, w_ref, o_ref, ob_ref = refs
    else:
        x_ref, w_ref, o_ref = refs
    acc = _dot(x_ref[...], w_ref[...])
    if mode == "logsig":
        o_ref[...] = _log_sigmoid(acc + b_ref[...])
    elif mode == "dual":
        o_ref[...] = acc.reshape(o_ref.shape)
        ob_ref[...] = acc.astype(BF16)
    elif mode == "scale":
        o_ref[...] = (acc * scale).astype(o_ref.dtype)
    else:
        o_ref[...] = acc.astype(o_ref.dtype)


def _proj(xb, w, *, cols=None, mode="plain", out_dtype=F32, scale=1.0, bias=None, name):
    m, k = xb.shape
    start, n = cols if cols is not None else (0, w.shape[-1])
    tm = _tile(m, 1024)
    resident = k * n * w.dtype.itemsize <= RESIDENT_WEIGHT_BYTES
    tn = n if resident else _tile(n, 1024)
    assert start % tn == 0
    j0 = start // tn
    if w.ndim == 3:
        assert w.shape[0] == 1
        w_block, w_map = (None, k, tn), lambda i, j: (0, 0, j0 + j)
    else:
        w_block, w_map = (k, tn), lambda i, j: (0, j0 + j)
    in_specs = [pl.BlockSpec((tm, k), lambda i, j: (i, 0)),
                _resident(w_block, w_map) if resident else pl.BlockSpec(w_block, w_map)]
    args = [xb, w]
    o_spec = pl.BlockSpec((tm, tn), lambda i, j: (i, j))
    if mode == "logsig":
        in_specs.append(pl.BlockSpec((1, tn), lambda i, j: (0, j)))
        args.append(bias)
    if mode == "dual":
        assert tn == n and n % ATT_HEAD_DIM == 0
        heads = n // ATT_HEAD_DIM
        out_shape = (jax.ShapeDtypeStruct((m, heads, ATT_HEAD_DIM), F32), jax.ShapeDtypeStruct((m, n), BF16))
        out_specs = (pl.BlockSpec((tm, heads, ATT_HEAD_DIM), lambda i, j: (i, 0, 0)), o_spec)
    else:
        out_shape = jax.ShapeDtypeStruct((m, n), out_dtype)
        out_specs = o_spec
    return pl.pallas_call(
        functools.partial(_proj_kernel, mode=mode, scale=scale),
        out_shape=out_shape,
        grid=(m // tm, n // tn),
        in_specs=in_specs,
        out_specs=out_specs,
        compiler_params=_cparams("parallel", "arbitrary"),
        name=name,
    )(*args)


CONV_PAD = SUBLANES
CONV_ROWS = 256


def _conv_kernel(x_ref, st_ref, w_ref, b_ref, act_ref, nc_ref, pad_ref, *, seq):
    lo = CONV_PAD - (CONV_WIDTH - 1)
    pad_ref[:, lo:CONV_PAD, :] = st_ref[...]
    pad_ref[:, CONV_PAD:CONV_PAD + seq, :] = x_ref[...]
    w = w_ref[...]
    b = b_ref[...]
    rc = min(seq, CONV_ROWS)
    for r0 in range(0, seq, rc):
        if x_ref.shape[0] == 1 and rc % SUBLANES == 0:
            win = pad_ref[0, r0:r0 + rc + CONV_PAD, :]
            acc = b + w[CONV_WIDTH - 1:CONV_WIDTH, :] * win[CONV_PAD:, :]
            for k in range(1, CONV_WIDTH):
                acc = acc + w[CONV_WIDTH - 1 - k:CONV_WIDTH - k, :] * pltpu.roll(win, k, axis=0)[CONV_PAD:, :]
            act_ref[0, r0:r0 + rc, :] = acc * jax.nn.sigmoid(acc)
        else:
            acc = b + w[0:1, :] * pad_ref[:, lo + r0:lo + r0 + rc, :]
            for k in range(1, CONV_WIDTH):
                acc = acc + w[k:k + 1, :] * pad_ref[:, lo + k + r0:lo + k + r0 + rc, :]
            act_ref[:, r0:r0 + rc, :] = acc * jax.nn.sigmoid(acc)
    nc_ref[...] = pad_ref[:, lo + seq:CONV_PAD + seq, :]


def _conv_silu(zx, conv_state, conv_w, conv_b, *, d_inner, name):
    bsz, seq, _ = zx.shape
    conv_dim = conv_w.shape[1]
    if seq >= SSD_CHUNK:
        bb, tc = 1, 512
    else:
        bb, tc = _tile(bsz, 16), 2048
    assert conv_dim % tc == 0 and d_inner % tc == 0
    off = d_inner // tc
    return pl.pallas_call(
        functools.partial(_conv_kernel, seq=seq),
        out_shape=(jax.ShapeDtypeStruct((bsz, seq, conv_dim), F32),
                   jax.ShapeDtypeStruct((bsz, CONV_WIDTH - 1, conv_dim), F32)),
        grid=(bsz // bb, conv_dim // tc),
        in_specs=[
            pl.BlockSpec((bb, seq, tc), lambda b, j: (b, 0, off + j)),
            pl.BlockSpec((bb, CONV_WIDTH - 1, tc), lambda b, j: (b, 0, j)),
            pl.BlockSpec((CONV_WIDTH, tc), lambda b, j: (0, j)),
            pl.BlockSpec((1, tc), lambda b, j: (0, j)),
        ],
        out_specs=(
            pl.BlockSpec((bb, seq, tc), lambda b, j: (b, 0, j)),
            pl.BlockSpec((bb, CONV_WIDTH - 1, tc), lambda b, j: (b, 0, j)),
        ),
        scratch_shapes=[pltpu.VMEM((bb, CONV_PAD + seq, tc), F32)],
        compiler_params=_cparams("parallel", "parallel"),
        name=name,
    )(zx, conv_state, conv_w, conv_b.reshape(1, conv_dim))


def _ssd_kernel(*refs, rows, cl, has_h0, d_inner, n_state):
    groups = SSM_GROUPS
    gw = d_inner // groups
    pairs = gw // LANES
    heads_per_group = gw // SSM_HEAD_DIM
    padded = rows < cl
    it = iter(refs)
    xs_ref, bm_ref, cm_ref, z_ref, dt_ref = next(it), next(it), next(it), next(it), next(it)
    dtb_ref, alog_ref, dsk_ref, nw_ref, tri_ref, exp_ref = (next(it), next(it), next(it), next(it), next(it), next(it))
    h0_ref = next(it) if has_h0 else None
    g_ref, hout_ref = next(it), next(it)
    ht_ref = next(it)
    if padded:
        xs_p, bm_p, cm_p, z_p, dt_p = next(it), next(it), next(it), next(it), next(it)

    c = pl.program_id(1)

    @pl.when(c == 0)
    def _init():
        if has_h0:
            for i in range(d_inner // LANES):
                ht_ref[:, i * LANES:(i + 1) * LANES] = h0_ref[0, i * LANES:(i + 1) * LANES, :].T
        else:
            ht_ref[...] = jnp.zeros_like(ht_ref)

    if padded:
        for src, dst in ((xs_ref, xs_p), (bm_ref, bm_p), (cm_ref, cm_p), (z_ref, z_p), (dt_ref, dt_p)):
            dst[...] = jnp.zeros_like(dst)
            dst[0:rows, :] = src[0]
        xs_v, bm_v, cm_v, z_v, dt_v = xs_p, bm_p, cm_p, z_p, dt_p
        rd = lambda ref, sl: ref[:, sl]
    else:
        xs_v, bm_v, cm_v, z_v, dt_v = xs_ref, bm_ref, cm_ref, z_ref, dt_ref
        rd = lambda ref, sl: ref[0, :, sl]

    full = slice(None)
    dt = _softplus(rd(dt_v, full) + dtb_ref[...])
    if padded:
        row = lax.broadcasted_iota(jnp.int32, dt.shape, 0)
        dt = jnp.where(row < rows, dt, 0.0)
    a = -jnp.exp(alog_ref[...])
    tri = tri_ref[...]
    a_cum = _sel_dot(tri, dt * a, 3)
    a_cum_t = a_cum.T
    dt_t = dt.T
    a_last = a_cum[cl - 1:cl, :]
    ea = jnp.exp(a_cum)
    de = jnp.exp(a_last - a_cum)

    li = lax.broadcasted_iota(jnp.int32, (cl, cl), 0)
    si = lax.broadcasted_iota(jnp.int32, (cl, cl), 1)
    causal = li >= si
    lane = lax.broadcasted_iota(jnp.int32, (cl, LANES), 1)
    first_head = lane < SSM_HEAD_DIM

    for g in range(groups):
        gs = slice(g * gw, (g + 1) * gw)
        ns = slice(g * n_state, (g + 1) * n_state)
        expand = exp_ref[:, gs]
        xs_g = rd(xs_v, gs)
        ea_e = _dot_sel(ea, expand, 2)
        dtde_e = _dot_sel(dt * de, expand, 2)
        bm_t = rd(bm_v, ns).T.astype(BF16)
        cm_b = rd(cm_v, ns).astype(BF16)
        cb = _dot(cm_b, bm_t)
        ht_g = ht_ref[:, gs]
        y_off = _dot(cm_b, ht_g.astype(BF16)) * ea_e
        y_parts = []
        for j in range(pairs):
            sc = []
            for e in range(2):
                h = g * heads_per_group + 2 * j + e
                seg = a_cum[:, h:h + 1] - a_cum_t[h:h + 1, :]
                dec = jnp.exp(jnp.where(causal, seg, -jnp.inf))
                sc.append((cb * dec * dt_t[h:h + 1, :]).astype(BF16))
            lhs = jnp.concatenate(sc, axis=1)
            xp = xs_g[:, j * LANES:(j + 1) * LANES]
            rhs = jnp.concatenate([jnp.where(first_head, xp, 0.0), jnp.where(first_head, 0.0, xp)],
                                  axis=0).astype(BF16)
            y_parts.append(_dot(lhs, rhs))
        y_g = jnp.concatenate(y_parts, axis=1) + y_off + dsk_ref[:, gs] * xs_g
        ht_ref[:, gs] = ea_e[cl - 1:cl, :] * ht_g + _dot(bm_t, (xs_g * dtde_e).astype(BF16))
        z_g = rd(z_v, gs)
        gz = y_g * (z_g * jax.nn.sigmoid(z_g))
        ms = jnp.mean(gz * gz, axis=-1, keepdims=True)
        out_g = gz * lax.rsqrt(ms + RMS_EPS) * nw_ref[:, gs]
        g_ref[0, :, gs] = out_g[0:rows, :].astype(BF16)

    @pl.when(c == pl.num_programs(1) - 1)
    def _fin():
        for i in range(d_inner // LANES):
            hout_ref[0, i * LANES:(i + 1) * LANES, :] = ht_ref[:, i * LANES:(i + 1) * LANES].T


def _ssd(zx, act, dt_raw, ssm0, dt_bias, a_log, d_skip, norm_w, *, d_inner, name):
    bsz, seq, _ = act.shape
    n_state = SSM_STATE
    heads = d_inner // SSM_HEAD_DIM
    gn = SSM_GROUPS * n_state
    assert heads <= LANES and d_inner % gn == 0 and gn % LANES == 0
    if seq % SSD_CHUNK == 0:
        rows = cl = SSD_CHUNK
    else:
        rows, cl = seq, -(-seq // BF16_ROWS) * BF16_ROWS
    nc = seq // rows
    has_h0 = ssm0 is not None

    def pad_heads(v):
        return jnp.pad(v.astype(F32), (0, LANES - heads)).reshape(1, LANES)

    tri = (jnp.arange(cl)[:, None] >= jnp.arange(cl)[None, :]).astype(BF16)
    expand = (jnp.arange(LANES)[:, None] == (jnp.arange(d_inner) // SSM_HEAD_DIM)[None, :]).astype(BF16)
    dsk = jnp.repeat(d_skip.astype(F32), SSM_HEAD_DIM).reshape(1, d_inner)
    nw = norm_w.astype(F32).reshape(1, d_inner)
    bc_off = d_inner // gn

    const = lambda shape: pl.BlockSpec(shape, lambda b, c: (0,) * len(shape))
    in_specs = [
        pl.BlockSpec((1, rows, d_inner), lambda b, c: (b, c, 0)),
        pl.BlockSpec((1, rows, gn), lambda b, c: (b, c, bc_off)),
        pl.BlockSpec((1, rows, gn), lambda b, c: (b, c, bc_off + 1)),
        pl.BlockSpec((1, rows, d_inner), lambda b, c: (b, c, 0)),
        pl.BlockSpec((1, rows, LANES), lambda b, c: (b, c, 0)),
        const((1, LANES)), const((1, LANES)), const((1, d_inner)), const((1, d_inner)),
        const((cl, cl)), const((LANES, d_inner)),
    ]
    args = [act, act, act, zx, dt_raw, pad_heads(dt_bias), pad_heads(a_log), dsk, nw, tri, expand]
    if has_h0:
        in_specs.append(pl.BlockSpec((1, d_inner, n_state), lambda b, c: (b, 0, 0)))
        args.append(ssm0.reshape(bsz, d_inner, n_state))
    scratch = [pltpu.VMEM((n_state, d_inner), F32)]
    if rows < cl:
        scratch += [pltpu.VMEM((cl, d_inner), F32), pltpu.VMEM((cl, gn), F32), pltpu.VMEM((cl, gn), F32),
                    pltpu.VMEM((cl, d_inner), F32), pltpu.VMEM((cl, LANES), F32)]
    g, hout = pl.pallas_call(
        functools.partial(_ssd_kernel, rows=rows, cl=cl, has_h0=has_h0, d_inner=d_inner, n_state=n_state),
        out_shape=(jax.ShapeDtypeStruct((bsz, seq, d_inner), BF16),
                   jax.ShapeDtypeStruct((bsz, d_inner, n_state), F32)),
        grid=(bsz, nc),
        in_specs=in_specs,
        out_specs=(pl.BlockSpec((1, rows, d_inner), lambda b, c: (b, c, 0)),
                   pl.BlockSpec((1, d_inner, n_state), lambda b, c: (b, 0, 0))),
        scratch_shapes=scratch,
        compiler_params=_cparams("parallel", "arbitrary"),
        name=name,
    )(*args)
    return g, hout.reshape(bsz, heads, SSM_HEAD_DIM, n_state)


def _mm_ln_kernel(x_ref, w_ref, res_ref, g_ref, b_ref, o_ref, ob_ref):
    y = _layer_norm(ALPHA * res_ref[...] + _dot(x_ref[...], w_ref[...]), g_ref[...], b_ref[...])
    o_ref[...] = y
    ob_ref[...] = y.astype(BF16)


def _mm_ln(xb, w, res, gamma, beta, layer, *, name):
    m, kdim = xb.shape
    n = w.shape[-1]
    assert w.shape[0] == 1 and kdim * n * w.dtype.itemsize <= RESIDENT_WEIGHT_BYTES
    tm = _tile(m, 512)
    row = lambda shape: pl.BlockSpec(shape, lambda i: (i, 0))
    vec = _resident((None, 1, n), lambda i: (layer, 0, 0))
    return pl.pallas_call(
        _mm_ln_kernel,
        out_shape=(jax.ShapeDtypeStruct((m, n), F32), jax.ShapeDtypeStruct((m, n), BF16)),
        grid=(m // tm,),
        in_specs=[
            row((tm, kdim)),
            _resident((None, kdim, n), lambda i: (0, 0, 0)),
            row((tm, n)),
            vec, vec,
        ],
        out_specs=(row((tm, n)), row((tm, n))),
        compiler_params=_cparams("parallel"),
        name=name,
    )(xb, w, res, gamma, beta)


def _ffn_kernel(x_ref, xb_ref, wg_ref, wu_ref, wd_ref, g_ref, b_ref, o_ref, ob_ref, acc_ref):
    f = pl.program_id(1)

    @pl.when(f == 0)
    def _():
        acc_ref[...] = jnp.zeros_like(acc_ref)

    xb = xb_ref[...]
    a = _dot(xb, wg_ref[...])
    u = _dot(xb, wu_ref[...])
    h = (a * jax.nn.sigmoid(a) * u).astype(BF16)
    acc_ref[...] += _dot(h, wd_ref[...])

    @pl.when(f == pl.num_programs(1) - 1)
    def _():
        y = _layer_norm(ALPHA * x_ref[...] + acc_ref[...], g_ref[...], b_ref[...])
        o_ref[...] = y
        ob_ref[...] = y.astype(BF16)


def _ffn(x, xb, wg, wu, wd, gamma, beta, layer, *, name):
    m, d = x.shape
    dff = wg.shape[-1]
    tm = _tile(m, 512)
    tf = _tile(dff, 512)
    row = lambda shape: pl.BlockSpec(shape, lambda i, f: (i, 0))
    vec = _resident((None, 1, d), lambda i, f: (layer, 0, 0))
    return pl.pallas_call(
        _ffn_kernel,
        out_shape=(jax.ShapeDtypeStruct((m, d), F32), jax.ShapeDtypeStruct((m, d), BF16)),
        grid=(m // tm, dff // tf),
        in_specs=[
            row((tm, d)), row((tm, d)),
            pl.BlockSpec((None, d, tf), lambda i, f: (layer, 0, f)),
            pl.BlockSpec((None, d, tf), lambda i, f: (layer, 0, f)),
            pl.BlockSpec((None, tf, d), lambda i, f: (layer, f, 0)),
            vec, vec,
        ],
        out_specs=(row((tm, d)), row((tm, d))),
        scratch_shapes=[pltpu.VMEM((tm, d), F32)],
        compiler_params=_cparams("parallel", "arbitrary"),
        name=name,
    )(x, xb, wg, wu, wd, gamma, beta)


def _ple_kernel(x_ref, xb_ref, p_ref, wg_ref, wp_ref, o_ref, ob_ref):
    gate = _dot(xb_ref[...], wg_ref[...])
    proj = _dot(p_ref[...].astype(BF16), wp_ref[...])
    y = x_ref[...] + jax.nn.sigmoid(gate) * proj
    o_ref[...] = y
    ob_ref[...] = y.astype(BF16)


def _ple(x, xb, p, w_gate, w_proj, layer, *, name):
    m, d = x.shape
    pd = p.shape[-1]
    assert d * d * w_gate.dtype.itemsize <= RESIDENT_WEIGHT_BYTES
    tm = _tile(m, 512)
    row = lambda width: pl.BlockSpec((tm, width), lambda i: (i, 0))
    return pl.pallas_call(
        _ple_kernel,
        out_shape=(jax.ShapeDtypeStruct((m, d), F32), jax.ShapeDtypeStruct((m, d), BF16)),
        grid=(m // tm,),
        in_specs=[
            row(d), row(d),
            pl.BlockSpec((None, tm, pd), lambda i: (layer, i, 0)),
            _resident((None, d, d), lambda i: (layer, 0, 0)),
            _resident((None, pd, d), lambda i: (layer, 0, 0)),
        ],
        out_specs=(row(d), row(d)),
        compiler_params=_cparams("parallel"),
        name=name,
    )(x, xb, p, w_gate, w_proj)


def _cumsum_kernel(x_ref, tri_ref, f_ref, ft_ref, *, seq, heads):
    cl = SSD_CHUNK
    tri = tri_ref[...]
    carry = jnp.zeros((1, LANES), F32)
    for c in range(seq // cl):
        rs = slice(c * cl, (c + 1) * cl)
        fc = _sel_dot(tri, x_ref[0, rs, :], 3) + carry
        f_ref[0, rs, :] = fc
        ft_ref[0, :, rs] = fc.T[0:heads, :]
        carry = fc[cl - 1:cl, :]


def _forget_cumsum(logf_pad, *, heads, name):
    bsz, seq, _ = logf_pad.shape
    cl = SSD_CHUNK
    tri = (jnp.arange(cl)[:, None] >= jnp.arange(cl)[None, :]).astype(BF16)
    return pl.pallas_call(
        functools.partial(_cumsum_kernel, seq=seq, heads=heads),
        out_shape=(jax.ShapeDtypeStruct((bsz, seq, LANES), F32), jax.ShapeDtypeStruct((bsz, heads, seq), F32)),
        grid=(bsz,),
        in_specs=[pl.BlockSpec((1, seq, LANES), lambda b: (b, 0, 0)), pl.BlockSpec((cl, cl), lambda b: (0, 0))],
        out_specs=(pl.BlockSpec((1, seq, LANES), lambda b: (b, 0, 0)), pl.BlockSpec((1, heads, seq), lambda b: (b, 0, 0))),
        compiler_params=_cparams("parallel"),
        name=name,
    )(logf_pad, tri)


ATT_BLOCK = 512


def _flash_kernel(q_ref, k_ref, v_ref, f_ref, ft_ref, o_ref, m_ref, l_ref, acc_ref, fq_ref, *, heads):
    qi = pl.program_id(1)
    ki = pl.program_id(2)
    tq = q_ref.shape[1]
    tk = k_ref.shape[1]
    hd = ATT_HEAD_DIM
    @pl.when(ki == 0)
    def _():
        m_ref[...] = jnp.full_like(m_ref, NEG_INF)
        l_ref[...] = jnp.zeros_like(l_ref)
        acc_ref[...] = jnp.zeros_like(acc_ref)
        f_q = f_ref[0]
        for h in range(heads):
            fq_ref[h] = jnp.broadcast_to(f_q[:, h:h + 1], (tq, LANES))

    rep = tk // LANES

    def lanes_to_tk(v):
        return jnp.concatenate([v] * rep, axis=1)

    def step(diagonal):
        f_k = ft_ref[0]
        if diagonal:
            mask = lax.broadcasted_iota(jnp.int32, (tq, tk), 0) >= lax.broadcasted_iota(jnp.int32, (tq, tk), 1)
        for h in range(heads):
            hs = slice(h * hd, (h + 1) * hd)
            s = _dot_nt(q_ref[0, :, hs], k_ref[0, :, hs])
            x = s + (lanes_to_tk(fq_ref[h]) - f_k[h:h + 1, :])
            if diagonal:
                x = jnp.where(mask, x, NEG_INF)
            m_prev = m_ref[h]
            m_new = jnp.maximum(m_prev, jnp.max(x, axis=-1, keepdims=True))
            p = jnp.exp(x - lanes_to_tk(m_new))
            alpha = jnp.exp(m_prev - m_new)
            l_ref[h] = alpha * l_ref[h] + jnp.sum(p, axis=-1, keepdims=True)
            acc_ref[h] = alpha * acc_ref[h] + _dot(p.astype(BF16), v_ref[0, :, hs])
            m_ref[h] = m_new

    @pl.when(ki < qi)
    def _():
        step(False)

    @pl.when(ki == qi)
    def _():
        step(True)
        for h in range(heads):
            o_ref[0, :, h * hd:(h + 1) * hd] = (acc_ref[h] / l_ref[h]).astype(o_ref.dtype)


def _flash_prompt(qb, kb, vb, f, ft, *, name):
    bsz, seq, d = qb.shape
    heads = d // ATT_HEAD_DIM
    t = _tile(seq, ATT_BLOCK)
    nb = seq // t
    return pl.pallas_call(
        functools.partial(_flash_kernel, heads=heads),
        out_shape=jax.ShapeDtypeStruct((bsz, seq, d), BF16),
        grid=(bsz, nb, nb),
        in_specs=[
            pl.BlockSpec((1, t, d), lambda b, i, j: (b, i, 0)),
            pl.BlockSpec((1, t, d), lambda b, i, j: (b, jnp.minimum(i, j), 0)),
            pl.BlockSpec((1, t, d), lambda b, i, j: (b, jnp.minimum(i, j), 0)),
            pl.BlockSpec((1, t, LANES), lambda b, i, j: (b, i, 0)),
            pl.BlockSpec((1, heads, t), lambda b, i, j: (b, 0, jnp.minimum(i, j))),
        ],
        out_specs=pl.BlockSpec((1, t, d), lambda b, i, j: (b, i, 0)),
        scratch_shapes=[pltpu.VMEM((heads, t, LANES), F32), pltpu.VMEM((heads, t, LANES), F32),
                        pltpu.VMEM((heads, t, ATT_HEAD_DIM), F32), pltpu.VMEM((heads, t, LANES), F32)],
        compiler_params=_cparams("parallel", "parallel", "arbitrary"),
        name=name,
    )(qb, kb, vb, f, ft)


PAGES_PER_STEP = 8


def _decode_kernel(pt_ref, q_ref, kn_ref, vn_ref, fn_ref, *rest, heads, tnew, pps):
    del pt_ref
    k_refs = rest[0:pps]
    v_refs = rest[pps:2 * pps]
    lf_refs = rest[2 * pps:3 * pps]
    a_ref, a2_ref, sur_ref, o_ref = rest[3 * pps:3 * pps + 4]
    m_ref, l_ref, acc_ref, carry_ref, cnrep_ref, madd_ref, cnt_ref = rest[3 * pps + 4:]
    j = pl.program_id(1)
    hd = ATT_HEAD_DIM
    page = k_refs[0].shape[1]
    nq = tnew * heads
    nk = page * heads
    fr = nk // LANES

    def widen(v, w):
        return v[:, 0:w] if w <= LANES else jnp.concatenate([v] * (w // LANES), axis=1)

    def update(xs, vbs):
        m_prev = m_ref[...]
        m_new = m_prev
        for x in xs:
            m_new = jnp.maximum(m_new, jnp.max(x, axis=-1, keepdims=True))
        alpha = jnp.exp(m_prev - m_new)
        l_new = alpha * l_ref[...]
        acc = alpha * acc_ref[...]
        for x, vb in zip(xs, vbs):
            p = jnp.exp(x - widen(m_new, x.shape[1]))
            l_new = l_new + jnp.sum(p, axis=-1, keepdims=True)
            acc = acc + _dot(p.astype(BF16), vb)
        l_ref[...] = l_new
        acc_ref[...] = acc
        m_ref[...] = m_new

    @pl.when(j == 0)
    def _init():
        m_ref[...] = jnp.full_like(m_ref, NEG_INF)
        l_ref[...] = jnp.zeros_like(l_ref)
        acc_ref[...] = jnp.zeros_like(acc_ref)
        carry_ref[...] = jnp.zeros_like(carry_ref)
        cnt_ref[...] = jnp.zeros_like(cnt_ref)
        run = jnp.zeros((1, LANES), F32)
        for t in range(tnew):
            run = run + fn_ref[0, t:t + 1, :]
            cnt_ref[t:t + 1, :] = run
        cn = cnt_ref[...]
        cn_t = cn.T
        cncol = jnp.concatenate([cn_t[0:heads, t:t + 1] for t in range(tnew)], axis=0)
        cnrep_ref[...] = jnp.broadcast_to(cncol, (nq, LANES))
        r = lax.broadcasted_iota(jnp.int32, (nq, nk), 0)
        c = lax.broadcasted_iota(jnp.int32, (nq, nk), 1)
        madd_ref[...] = jnp.where(r % heads == c % heads, jnp.broadcast_to(cncol, (nq, nk)), NEG_INF)
        s = _dot_nt(q_ref[0], kn_ref[0].astype(BF16))
        brow = jnp.concatenate([cn[t:t + 1, 0:heads] for t in range(tnew)], axis=1)
        x = s + (cnrep_ref[:, 0:nq] - brow)
        rr = lax.broadcasted_iota(jnp.int32, (nq, nq), 0)
        cc = lax.broadcasted_iota(jnp.int32, (nq, nq), 1)
        ok = jnp.logical_and(rr % heads == cc % heads, cc // heads <= rr // heads)
        update([jnp.where(ok, x, NEG_INF)], [vn_ref[0].astype(BF16)])

    a_sel = a_ref[...]
    a2_sel = a2_ref[...]
    sur = sur_ref[...]
    q = q_ref[0]
    xs, vbs = [], []
    for i in range(pps):
        lf = lf_refs[i][0]
        rowtot = _dot_sel(lf, a2_sel, 3)
        suf = _dot_sel(lf, a_sel, 3) + _sel_dot(sur, rowtot, 3) + carry_ref[0:1, :]
        carry_ref[0:1, :] = carry_ref[0:1, :] + jnp.sum(rowtot, axis=0, keepdims=True)
        brow = jnp.concatenate([suf[rw:rw + 1, :] for rw in range(fr)], axis=1)
        k2 = k_refs[i][0].reshape(nk, hd).astype(BF16)
        xs.append(_dot_nt(q, k2) + (brow + madd_ref[...]))
        vbs.append(v_refs[i][0].reshape(nk, hd).astype(BF16))
    update(xs, vbs)

    @pl.when(j == pl.num_programs(1) - 1)
    def _fin():
        o_ref[0] = acc_ref[...] / l_ref[...]


def _decode_attention(qb, k_new, v_new, f_new, cache_k, cache_v, cache_logf, page_table, *, name):
    bsz, tnew, d = qb.shape
    n_pool, page, heads, hd = cache_k.shape
    n_pages = page_table.shape[1]
    pps = PAGES_PER_STEP
    nq = tnew * heads
    nk = page * heads
    fr = nk // LANES
    assert n_pages % pps == 0 and tnew <= SUBLANES and hd == ATT_HEAD_DIM and heads * hd == d
    assert LANES % heads == 0 and nk % LANES == 0
    lane = jnp.arange(LANES)
    same_head = (lane[:, None] % heads) == (lane[None, :] % heads)
    a2_sel = same_head.astype(BF16)
    a_sel = jnp.logical_and(same_head, lane[:, None] // heads > lane[None, :] // heads).astype(BF16)
    sur = (jnp.arange(fr)[None, :] > jnp.arange(fr)[:, None]).astype(BF16)
    lf_flat = cache_logf.reshape(n_pool, fr, LANES)

    def slot(b, j, pt, i):
        return pt[b, n_pages - 1 - (j * pps + i)]

    def kv_spec(i):
        return pl.BlockSpec((1, page, heads, hd), lambda b, j, pt: (slot(b, j, pt, i), 0, 0, 0))

    def lf_spec(i):
        return pl.BlockSpec((1, fr, LANES), lambda b, j, pt: (slot(b, j, pt, i), 0, 0))

    per_b = lambda shape: pl.BlockSpec(shape, lambda b, j, pt: (b, 0, 0))
    const = lambda shape: pl.BlockSpec(shape, lambda b, j, pt: (0, 0))
    in_specs = [per_b((1, nq, hd)), per_b((1, nq, hd)), per_b((1, nq, hd)), per_b((1, tnew, LANES))]
    in_specs += [kv_spec(i) for i in range(pps)]
    in_specs += [kv_spec(i) for i in range(pps)]
    in_specs += [lf_spec(i) for i in range(pps)]
    in_specs += [const((LANES, LANES)), const((LANES, LANES)), const((fr, fr))]
    grid_spec = pltpu.PrefetchScalarGridSpec(
        num_scalar_prefetch=1,
        grid=(bsz, n_pages // pps),
        in_specs=in_specs,
        out_specs=per_b((1, nq, hd)),
        scratch_shapes=[
            pltpu.VMEM((nq, LANES), F32),
            pltpu.VMEM((nq, LANES), F32),
            pltpu.VMEM((nq, hd), F32),
            pltpu.VMEM((SUBLANES, LANES), F32),
            pltpu.VMEM((nq, LANES), F32),
            pltpu.VMEM((nq, nk), F32),
            pltpu.VMEM((SUBLANES, LANES), F32),
        ],
    )
    out = pl.pallas_call(
        functools.partial(_decode_kernel, heads=heads, tnew=tnew, pps=pps),
        out_shape=jax.ShapeDtypeStruct((bsz, nq, hd), F32),
        grid_spec=grid_spec,
        compiler_params=_cparams("parallel", "arbitrary"),
        name=name,
    )(page_table, qb.reshape(bsz, nq, hd), k_new, v_new, f_new,
      *([cache_k] * pps), *([cache_v] * pps), *([lf_flat] * pps), a_sel, a2_sel, sur)
    return out.reshape(bsz, tnew, d)


def _run(x, p, conv0, ssm0, w, tag, attend):
    bsz, seq, d = x.shape
    m = bsz * seq
    x2 = x.reshape(m, d)
    xb = x2.astype(BF16)
    d_inner = w["d_inner"]
    heads = d // ATT_HEAD_DIM

    p = p.reshape(p.shape[0], m, p.shape[-1])
    ffn = lambda x2, xb, layer: _ffn(x2, xb, w["wg"], w["wu"], w["wd"], w["ln2_g"], w["ln2_b"], layer,
                                     name=f"ffn{layer}_{tag}")
    ple = lambda x2, xb, layer: _ple(x2, xb, p, w["ple_g"], w["ple_p"], layer, name=f"ple{layer}_{tag}")

    zx = _proj(xb, w["w_in"], cols=(0, w["zx_dim"]), name=f"in_proj_{tag}")
    dt_raw = _proj(xb, w["w_dt"], name=f"dt_proj_{tag}")
    act, new_conv = _conv_silu(zx.reshape(bsz, seq, -1), conv0, w["conv_w"], w["conv_b"], d_inner=d_inner,
                               name=f"conv_{tag}")
    g, new_ssm = _ssd(zx.reshape(bsz, seq, -1), act, dt_raw.reshape(bsz, seq, LANES), ssm0, w["dt_bias"], w["a_log"],
                      w["d_skip"], w["norm_w"], d_inner=d_inner, name=f"ssd_{tag}")
    x2, xb = _mm_ln(g.reshape(m, d_inner), w["w_out"], x2, w["ln1_g"], w["ln1_b"], 0, name=f"out_proj_ln_{tag}")
    x2, xb = ffn(x2, xb, 0)
    x2, xb = ple(x2, xb, 0)

    k, kb = _proj(xb, w["w_kv"], cols=(0, d), mode="dual", name=f"k_proj_{tag}")
    v, vb = _proj(xb, w["w_kv"], cols=(d, d), mode="dual", name=f"v_proj_{tag}")
    logf_pad = _proj(xb, w["w_f"], mode="logsig", bias=w["b_f"], name=f"f_proj_{tag}")

    qb = _proj(xb, w["w_q"], mode="scale", scale=ATT_HEAD_DIM ** -0.5, out_dtype=BF16, name=f"q_proj_{tag}")
    rows_heads = (bsz, seq * heads, ATT_HEAD_DIM)
    o = attend(qb.reshape(bsz, seq, d), k.reshape(rows_heads), v.reshape(rows_heads), kb.reshape(bsz, seq, d),
               vb.reshape(bsz, seq, d), logf_pad.reshape(bsz, seq, LANES))
    x2, xb = _mm_ln(o.reshape(m, d).astype(BF16), w["w_o"], x2, w["ln1_g"], w["ln1_b"], 1, name=f"o_proj_ln_{tag}")
    x2, xb = ffn(x2, xb, 1)
    y, _ = ple(x2, xb, 1)

    return (y.reshape(bsz, seq, d), new_conv[None], new_ssm[None],
            k.reshape(bsz, seq, heads, ATT_HEAD_DIM), v.reshape(bsz, seq, heads, ATT_HEAD_DIM),
            logf_pad.reshape(bsz, seq, LANES)[..., :heads])


def kernel(x_prompt, x_sample, p_prompt, p_sample, state_conv, state_ssm, cache_k, cache_v, cache_logf, page_table, a_w_in, a_conv_w, a_conv_b, a_dt_bias, a_log, a_d, a_norm_w, a_w_out, kv_w, kv_b_f, b_w_q, b_w_o, ln1_g, ln1_b, ffn_w_gate, ffn_w_up, ffn_w_down, ln2_g, ln2_b, ple_w_gate, ple_w_proj):
    assert a_w_in.shape[0] == 1 and b_w_q.shape[0] == 1, "one SSD layer followed by one attention layer"
    d = x_prompt.shape[-1]
    heads = d // ATT_HEAD_DIM
    d_att = heads * ATT_HEAD_DIM
    conv_dim = a_conv_w.shape[-1]
    d_inner = conv_dim - 2 * SSM_GROUPS * SSM_STATE
    ssm_heads = d_inner // SSM_HEAD_DIM
    zx_dim = d_inner + conv_dim

    def pad_cols(wm):
        return jnp.pad(wm, ((0, 0), (0, LANES - wm.shape[1])))

    bf = lambda t: t.astype(BF16)
    per_layer_vec = lambda t: t.astype(F32).reshape(t.shape[0], 1, t.shape[1])
    w = dict(
        d_inner=d_inner, zx_dim=zx_dim,
        w_in=bf(a_w_in), w_dt=bf(pad_cols(a_w_in[0][:, zx_dim:])),
        conv_w=a_conv_w[0], conv_b=a_conv_b[0], dt_bias=a_dt_bias[0], a_log=a_log[0], d_skip=a_d[0],
        norm_w=a_norm_w[0], w_out=bf(a_w_out),
        w_kv=bf(kv_w), w_f=bf(pad_cols(kv_w[:, 2 * d_att:])),
        b_f=jnp.pad(kv_b_f.astype(F32), (0, LANES - heads)).reshape(1, LANES),
        w_q=bf(b_w_q), w_o=bf(b_w_o),
        ln1_g=per_layer_vec(ln1_g), ln1_b=per_layer_vec(ln1_b), ln2_g=per_layer_vec(ln2_g), ln2_b=per_layer_vec(ln2_b),
        wg=bf(ffn_w_gate), wu=bf(ffn_w_up), wd=bf(ffn_w_down), ple_g=bf(ple_w_gate), ple_p=bf(ple_w_proj),
    )
    assert ssm_heads * SSM_HEAD_DIM == d_inner

    def attend_prompt(qb, k, v, kb, vb, logf_pad):
        f, ft = _forget_cumsum(logf_pad, heads=heads, name="forget_cumsum")
        return _flash_prompt(qb, kb, vb, f, ft, name="flash_prompt")

    def attend_sample(qb, k, v, kb, vb, logf_pad):
        return _decode_attention(qb, k, v, logf_pad, cache_k, cache_v, cache_logf, page_table, name="decode_attn")

    bp = x_prompt.shape[0]
    conv0_p = jnp.zeros((bp, CONV_WIDTH - 1, conv_dim), F32)
    outs_p = _run(x_prompt, p_prompt, conv0_p, None, w, "p", attend_prompt)
    outs_s = _run(x_sample, p_sample, state_conv[0], state_ssm[0], w, "s", attend_sample)
    y_p, conv_p, ssm_p, k_p, v_p, f_p = outs_p
    y_s, conv_s, ssm_s, k_s, v_s, f_s = outs_s
    return (y_p, y_s, conv_p, ssm_p, k_p, v_p, f_p, conv_s, ssm_s, k_s, v_s, f_s)
```

```python
import functools

import jax
import jax.numpy as jnp
from jax import lax
from jax.experimental import pallas as pl
from jax.experimental.pallas import tpu as pltpu

F32 = jnp.float32
BF16 = jnp.bfloat16

DEPTH = 2
SSM_HEAD_DIM = 64
SSM_GROUPS = 8
SSM_STATE = 128
CONV_WIDTH = 4
SSD_CHUNK = 128
ATT_HEAD_DIM = 128
ALPHA = (2 * DEPTH) ** 0.25
LN_EPS = 1e-5
RMS_EPS = 1e-5
NEG_INF = -1e30

LANES = 128
SUBLANES = 8
BF16_ROWS = 2 * SUBLANES
V7X_VMEM_BYTES = 64 * 1024 * 1024
VMEM_LIMIT = V7X_VMEM_BYTES * 7 // 8


def _cparams(*sem):
    return pltpu.CompilerParams(dimension_semantics=sem, vmem_limit_bytes=VMEM_LIMIT)


def _dot(a, b):
    return jnp.dot(a, b, preferred_element_type=F32)


def _dot_nt(a, b):
    return lax.dot_general(a, b, (((1,), (1,)), ((), ())), preferred_element_type=F32)


def _split_bf16(x, parts):
    out = []
    r = x
    for i in range(parts):
        p = r.astype(BF16)
        out.append(p)
        if i + 1 < parts:
            r = r - p.astype(F32)
    return out


def _sel_dot(sel, x, parts):
    acc = None
    for p in _split_bf16(x, parts):
        t = _dot(sel, p)
        acc = t if acc is None else acc + t
    return acc


def _dot_sel(x, sel, parts):
    acc = None
    for p in _split_bf16(x, parts):
        t = _dot(p, sel)
        acc = t if acc is None else acc + t
    return acc


def _softplus(x):
    return jnp.maximum(x, 0.0) + jnp.log1p(jnp.exp(-jnp.abs(x)))


def _log_sigmoid(x):
    return jnp.minimum(x, 0.0) - jnp.log1p(jnp.exp(-jnp.abs(x)))


def _layer_norm(v, g, b):
    mu = jnp.mean(v, axis=-1, keepdims=True)
    d = v - mu
    var = jnp.mean(d * d, axis=-1, keepdims=True)
    return d * lax.rsqrt(var + LN_EPS) * g + b


def _tile(n, cap):
    t = min(n, cap)
    assert n % t == 0, (n, cap)
    return t


RESIDENT_WEIGHT_BYTES = V7X_VMEM_BYTES // 4


def _resident(shape, index_map):
    return pl.BlockSpec(shape, index_map, pipeline_mode=pl.Buffered(1))


def _proj_kernel(*refs, mode, scale):
    if mode == "cast_x":
        x_ref, w_ref, o_ref, xbo_ref, xb_ref = refs

        @pl.when(pl.program_id(1) == 0)
        def _():
            xb = x_ref[...].astype(BF16)
            xb_ref[...] = xb
            xbo_ref[...] = xb

        o_ref[...] = _dot(xb_ref[...], w_ref[...])
        return
    if mode == "logsig":
        x_ref, w_ref, b_ref, o_ref = refs
    elif mode == "dual":
        x_ref, w_ref, o_ref, ob_ref = refs
    else:
        x_ref, w_ref, o_ref = refs
    acc = _dot(x_ref[...], w_ref[...])
    if mode == "logsig":
        o_ref[...] = _log_sigmoid(acc + b_ref[...])
    elif mode == "dual":
        o_ref[...] = acc.reshape(o_ref.shape)
        ob_ref[...] = acc.astype(BF16)
    elif mode == "scale":
        o_ref[...] = (acc * scale).astype(o_ref.dtype)
    else:
        o_ref[...] = acc.astype(o_ref.dtype)


def _proj(xb, w, *, cols=None, mode="plain", out_dtype=F32, scale=1.0, bias=None, name):
    m, k = xb.shape
    start, n = cols if cols is not None else (0, w.shape[-1])
    tm = _tile(m, 1024)
    resident = k * n * w.dtype.itemsize <= RESIDENT_WEIGHT_BYTES
    tn = n if resident else _tile(n, 1024)
    assert start % tn == 0
    j0 = start // tn
    if w.ndim == 3:
        assert w.shape[0] == 1
        w_block, w_map = (None, k, tn), lambda i, j: (0, 0, j0 + j)
    else:
        w_block, w_map = (k, tn), lambda i, j: (0, j0 + j)
    in_specs = [pl.BlockSpec((tm, k), lambda i, j: (i, 0)),
                _resident(w_block, w_map) if resident else pl.BlockSpec(w_block, w_map)]
    args = [xb, w]
    o_spec = pl.BlockSpec((tm, tn), lambda i, j: (i, j))
    if mode == "logsig":
        in_specs.append(pl.BlockSpec((1, tn), lambda i, j: (0, j)))
        args.append(bias)
    if mode == "dual":
        assert tn == n and n % ATT_HEAD_DIM == 0
        heads = n // ATT_HEAD_DIM
        out_shape = (jax.ShapeDtypeStruct((m, heads, ATT_HEAD_DIM), F32), jax.ShapeDtypeStruct((m, n), BF16))
        out_specs = (pl.BlockSpec((tm, heads, ATT_HEAD_DIM), lambda i, j: (i, 0, 0)), o_spec)
    elif mode == "cast_x":
        assert xb.dtype == F32
        out_shape = (jax.ShapeDtypeStruct((m, n), F32), jax.ShapeDtypeStruct((m, k), BF16))
        out_specs = (o_spec, pl.BlockSpec((tm, k), lambda i, j: (i, 0)))
    else:
        out_shape = jax.ShapeDtypeStruct((m, n), out_dtype)
        out_specs = o_spec
    return pl.pallas_call(
        functools.partial(_proj_kernel, mode=mode, scale=scale),
        out_shape=out_shape,
        grid=(m // tm, n // tn),
        in_specs=in_specs,
        out_specs=out_specs,
        scratch_shapes=[pltpu.VMEM((tm, k), BF16)] if mode == "cast_x" else [],
        compiler_params=_cparams("parallel", "arbitrary"),
        name=name,
    )(*args)


CONV_PAD = SUBLANES
CONV_ROWS = 256


def _conv_kernel(x_ref, st_ref, w_ref, b_ref, act_ref, nc_ref, pad_ref, *, seq):
    lo = CONV_PAD - (CONV_WIDTH - 1)
    pad_ref[:, lo:CONV_PAD, :] = st_ref[...]
    pad_ref[:, CONV_PAD:CONV_PAD + seq, :] = x_ref[...]
    w = w_ref[...]
    b = b_ref[...]
    rc = min(seq, CONV_ROWS)
    for r0 in range(0, seq, rc):
        if x_ref.shape[0] == 1 and rc % SUBLANES == 0:
            win = pad_ref[0, r0:r0 + rc + CONV_PAD, :]
            acc = b + w[CONV_WIDTH - 1:CONV_WIDTH, :] * win[CONV_PAD:, :]
            for k in range(1, CONV_WIDTH):
                acc = acc + w[CONV_WIDTH - 1 - k:CONV_WIDTH - k, :] * pltpu.roll(win, k, axis=0)[CONV_PAD:, :]
            act_ref[0, r0:r0 + rc, :] = acc * jax.nn.sigmoid(acc)
        else:
            acc = b + w[0:1, :] * pad_ref[:, lo + r0:lo + r0 + rc, :]
            for k in range(1, CONV_WIDTH):
                acc = acc + w[k:k + 1, :] * pad_ref[:, lo + k + r0:lo + k + r0 + rc, :]
            act_ref[:, r0:r0 + rc, :] = acc * jax.nn.sigmoid(acc)
    nc_ref[...] = pad_ref[:, lo + seq:CONV_PAD + seq, :]


def _conv_silu(zx, conv_state, conv_w, conv_b, *, d_inner, name):
    bsz, seq, _ = zx.shape
    conv_dim = conv_w.shape[1]
    if seq >= SSD_CHUNK:
        bb, tc = 1, 512
    else:
        bb, tc = _tile(bsz, 16), 2048
    assert conv_dim % tc == 0 and d_inner % tc == 0
    off = d_inner // tc
    return pl.pallas_call(
        functools.partial(_conv_kernel, seq=seq),
        out_shape=(jax.ShapeDtypeStruct((bsz, seq, conv_dim), F32),
                   jax.ShapeDtypeStruct((bsz, CONV_WIDTH - 1, conv_dim), F32)),
        grid=(bsz // bb, conv_dim // tc),
        in_specs=[
            pl.BlockSpec((bb, seq, tc), lambda b, j: (b, 0, off + j)),
            pl.BlockSpec((bb, CONV_WIDTH - 1, tc), lambda b, j: (b, 0, j)),
            pl.BlockSpec((CONV_WIDTH, tc), lambda b, j: (0, j)),
            pl.BlockSpec((1, tc), lambda b, j: (0, j)),
        ],
        out_specs=(
            pl.BlockSpec((bb, seq, tc), lambda b, j: (b, 0, j)),
            pl.BlockSpec((bb, CONV_WIDTH - 1, tc), lambda b, j: (b, 0, j)),
        ),
        scratch_shapes=[pltpu.VMEM((bb, CONV_PAD + seq, tc), F32)],
        compiler_params=_cparams("parallel", "parallel"),
        name=name,
    )(zx, conv_state, conv_w, conv_b.reshape(1, conv_dim))


def _ssd_kernel(*refs, rows, cl, has_h0, d_inner, n_state):
    groups = SSM_GROUPS
    gw = d_inner // groups
    pairs = gw // LANES
    heads_per_group = gw // SSM_HEAD_DIM
    padded = rows < cl
    it = iter(refs)
    xs_ref, bm_ref, cm_ref, z_ref, dt_ref = next(it), next(it), next(it), next(it), next(it)
    dtb_ref, alog_ref, dsk_ref, nw_ref, tri_ref, exp_ref = (next(it), next(it), next(it), next(it), next(it), next(it))
    h0_ref = next(it) if has_h0 else None
    g_ref, hout_ref = next(it), next(it)
    ht_ref = next(it)
    if padded:
        xs_p, bm_p, cm_p, z_p, dt_p = next(it), next(it), next(it), next(it), next(it)

    c = pl.program_id(1)

    @pl.when(c == 0)
    def _init():
        if has_h0:
            for i in range(d_inner // LANES):
                ht_ref[:, i * LANES:(i + 1) * LANES] = h0_ref[0, i * LANES:(i + 1) * LANES, :].T
        else:
            ht_ref[...] = jnp.zeros_like(ht_ref)

    if padded:
        for src, dst in ((xs_ref, xs_p), (bm_ref, bm_p), (cm_ref, cm_p), (z_ref, z_p), (dt_ref, dt_p)):
            dst[...] = jnp.zeros_like(dst)
            dst[0:rows, :] = src[0]
        xs_v, bm_v, cm_v, z_v, dt_v = xs_p, bm_p, cm_p, z_p, dt_p
        rd = lambda ref, sl: ref[:, sl]
    else:
        xs_v, bm_v, cm_v, z_v, dt_v = xs_ref, bm_ref, cm_ref, z_ref, dt_ref
        rd = lambda ref, sl: ref[0, :, sl]

    full = slice(None)
    dt = _softplus(rd(dt_v, full) + dtb_ref[...])
    if padded:
        row = lax.broadcasted_iota(jnp.int32, dt.shape, 0)
        dt = jnp.where(row < rows, dt, 0.0)
    a = -jnp.exp(alog_ref[...])
    tri = tri_ref[...]
    a_cum = _sel_dot(tri, dt * a, 3)
    a_cum_t = a_cum.T
    dt_t = dt.T
    a_last = a_cum[cl - 1:cl, :]
    ea = jnp.exp(a_cum)
    de = jnp.exp(a_last - a_cum)

    li = lax.broadcasted_iota(jnp.int32, (cl, cl), 0)
    si = lax.broadcasted_iota(jnp.int32, (cl, cl), 1)
    causal = li >= si
    lane = lax.broadcasted_iota(jnp.int32, (cl, LANES), 1)
    first_head = lane < SSM_HEAD_DIM

    for g in range(groups):
        gs = slice(g * gw, (g + 1) * gw)
        ns = slice(g * n_state, (g + 1) * n_state)
        expand = exp_ref[:, gs]
        xs_g = rd(xs_v, gs)
        ea_e = _dot_sel(ea, expand, 2)
        dtde_e = _dot_sel(dt * de, expand, 2)
        bm_t = rd(bm_v, ns).T.astype(BF16)
        cm_b = rd(cm_v, ns).astype(BF16)
        cb = _dot(cm_b, bm_t)
        ht_g = ht_ref[:, gs]
        y_off = _dot(cm_b, ht_g.astype(BF16)) * ea_e
        y_parts = []
        for j in range(pairs):
            sc = []
            for e in range(2):
                h = g * heads_per_group + 2 * j + e
                seg = a_cum[:, h:h + 1] - a_cum_t[h:h + 1, :]
                dec = jnp.exp(jnp.where(causal, seg, -jnp.inf))
                sc.append((cb * dec * dt_t[h:h + 1, :]).astype(BF16))
            lhs = jnp.concatenate(sc, axis=1)
            xp = xs_g[:, j * LANES:(j + 1) * LANES]
            rhs = jnp.concatenate([jnp.where(first_head, xp, 0.0), jnp.where(first_head, 0.0, xp)],
                                  axis=0).astype(BF16)
            y_parts.append(_dot(lhs, rhs))
        y_g = jnp.concatenate(y_parts, axis=1) + y_off + dsk_ref[:, gs] * xs_g
        ht_ref[:, gs] = ea_e[cl - 1:cl, :] * ht_g + _dot(bm_t, (xs_g * dtde_e).astype(BF16))
        z_g = rd(z_v, gs)
        gz = y_g * (z_g * jax.nn.sigmoid(z_g))
        ms = jnp.mean(gz * gz, axis=-1, keepdims=True)
        out_g = gz * lax.rsqrt(ms + RMS_EPS) * nw_ref[:, gs]
        g_ref[0, :, gs] = out_g[0:rows, :].astype(BF16)

    @pl.when(c == pl.num_programs(1) - 1)
    def _fin():
        for i in range(d_inner // LANES):
            hout_ref[0, i * LANES:(i + 1) * LANES, :] = ht_ref[:, i * LANES:(i + 1) * LANES].T


def _ssd(zx, act, dt_raw, ssm0, dt_bias, a_log, d_skip, norm_w, *, d_inner, name):
    bsz, seq, _ = act.shape
    n_state = SSM_STATE
    heads = d_inner // SSM_HEAD_DIM
    gn = SSM_GROUPS * n_state
    assert heads <= LANES and d_inner % gn == 0 and gn % LANES == 0
    if seq % SSD_CHUNK == 0:
        rows = cl = SSD_CHUNK
    else:
        rows, cl = seq, -(-seq // BF16_ROWS) * BF16_ROWS
    nc = seq // rows
    has_h0 = ssm0 is not None

    def pad_heads(v):
        return jnp.pad(v.astype(F32), (0, LANES - heads)).reshape(1, LANES)

    tri = (jnp.arange(cl)[:, None] >= jnp.arange(cl)[None, :]).astype(BF16)
    expand = (jnp.arange(LANES)[:, None] == (jnp.arange(d_inner) // SSM_HEAD_DIM)[None, :]).astype(BF16)
    dsk = jnp.repeat(d_skip.astype(F32), SSM_HEAD_DIM).reshape(1, d_inner)
    nw = norm_w.astype(F32).reshape(1, d_inner)
    bc_off = d_inner // gn

    const = lambda shape: pl.BlockSpec(shape, lambda b, c: (0,) * len(shape))
    in_specs = [
        pl.BlockSpec((1, rows, d_inner), lambda b, c: (b, c, 0)),
        pl.BlockSpec((1, rows, gn), lambda b, c: (b, c, bc_off)),
        pl.BlockSpec((1, rows, gn), lambda b, c: (b, c, bc_off + 1)),
        pl.BlockSpec((1, rows, d_inner), lambda b, c: (b, c, 0)),
        pl.BlockSpec((1, rows, LANES), lambda b, c: (b, c, 0)),
        const((1, LANES)), const((1, LANES)), const((1, d_inner)), const((1, d_inner)),
        const((cl, cl)), const((LANES, d_inner)),
    ]
    args = [act, act, act, zx, dt_raw, pad_heads(dt_bias), pad_heads(a_log), dsk, nw, tri, expand]
    if has_h0:
        in_specs.append(pl.BlockSpec((1, d_inner, n_state), lambda b, c: (b, 0, 0)))
        args.append(ssm0.reshape(bsz, d_inner, n_state))
    scratch = [pltpu.VMEM((n_state, d_inner), F32)]
    if rows < cl:
        scratch += [pltpu.VMEM((cl, d_inner), F32), pltpu.VMEM((cl, gn), F32), pltpu.VMEM((cl, gn), F32),
                    pltpu.VMEM((cl, d_inner), F32), pltpu.VMEM((cl, LANES), F32)]
    g, hout = pl.pallas_call(
        functools.partial(_ssd_kernel, rows=rows, cl=cl, has_h0=has_h0, d_inner=d_inner, n_state=n_state),
        out_shape=(jax.ShapeDtypeStruct((bsz, seq, d_inner), BF16),
                   jax.ShapeDtypeStruct((bsz, d_inner, n_state), F32)),
        grid=(bsz, nc),
        in_specs=in_specs,
        out_specs=(pl.BlockSpec((1, rows, d_inner), lambda b, c: (b, c, 0)),
                   pl.BlockSpec((1, d_inner, n_state), lambda b, c: (b, 0, 0))),
        scratch_shapes=scratch,
        compiler_params=_cparams("parallel", "arbitrary"),
        name=name,
    )(*args)
    return g, hout.reshape(bsz, heads, SSM_HEAD_DIM, n_state)


def _mm_ln_kernel(x_ref, w_ref, res_ref, g_ref, b_ref, o_ref, ob_ref):
    y = _layer_norm(ALPHA * res_ref[...] + _dot(x_ref[...], w_ref[...]), g_ref[...], b_ref[...])
    o_ref[...] = y
    ob_ref[...] = y.astype(BF16)


def _mm_ln(xb, w, res, gamma, beta, layer, *, name):
    m, kdim = xb.shape
    n = w.shape[-1]
    assert w.shape[0] == 1 and kdim * n * w.dtype.itemsize <= RESIDENT_WEIGHT_BYTES
    tm = _tile(m, 512)
    row = lambda shape: pl.BlockSpec(shape, lambda i: (i, 0))
    vec = _resident((None, 1, n), lambda i: (layer, 0, 0))
    return pl.pallas_call(
        _mm_ln_kernel,
        out_shape=(jax.ShapeDtypeStruct((m, n), F32), jax.ShapeDtypeStruct((m, n), BF16)),
        grid=(m // tm,),
        in_specs=[
            row((tm, kdim)),
            _resident((None, kdim, n), lambda i: (0, 0, 0)),
            row((tm, n)),
            vec, vec,
        ],
        out_specs=(row((tm, n)), row((tm, n))),
        compiler_params=_cparams("parallel"),
        name=name,
    )(xb, w, res, gamma, beta)


def _ffn_kernel(x_ref, xb_ref, wg_ref, wu_ref, wd_ref, g_ref, b_ref, o_ref, ob_ref, acc_ref):
    f = pl.program_id(1)

    @pl.when(f == 0)
    def _():
        acc_ref[...] = jnp.zeros_like(acc_ref)

    xb = xb_ref[...]
    a = _dot(xb, wg_ref[...])
    u = _dot(xb, wu_ref[...])
    h = (a * jax.nn.sigmoid(a) * u).astype(BF16)
    acc_ref[...] += _dot(h, wd_ref[...])

    @pl.when(f == pl.num_programs(1) - 1)
    def _():
        y = _layer_norm(ALPHA * x_ref[...] + acc_ref[...], g_ref[...], b_ref[...])
        o_ref[...] = y
        ob_ref[...] = y.astype(BF16)


def _ffn(x, xb, wg, wu, wd, gamma, beta, layer, *, name):
    m, d = x.shape
    dff = wg.shape[-1]
    tm = _tile(m, 512)
    tf = _tile(dff, 512)
    row = lambda shape: pl.BlockSpec(shape, lambda i, f: (i, 0))
    vec = _resident((None, 1, d), lambda i, f: (layer, 0, 0))
    return pl.pallas_call(
        _ffn_kernel,
        out_shape=(jax.ShapeDtypeStruct((m, d), F32), jax.ShapeDtypeStruct((m, d), BF16)),
        grid=(m // tm, dff // tf),
        in_specs=[
            row((tm, d)), row((tm, d)),
            pl.BlockSpec((None, d, tf), lambda i, f: (layer, 0, f)),
            pl.BlockSpec((None, d, tf), lambda i, f: (layer, 0, f)),
            pl.BlockSpec((None, tf, d), lambda i, f: (layer, f, 0)),
            vec, vec,
        ],
        out_specs=(row((tm, d)), row((tm, d))),
        scratch_shapes=[pltpu.VMEM((tm, d), F32)],
        compiler_params=_cparams("parallel", "arbitrary"),
        name=name,
    )(x, xb, wg, wu, wd, gamma, beta)


def _ple_kernel(x_ref, xb_ref, p_ref, wg_ref, wp_ref, o_ref, ob_ref):
    gate = _dot(xb_ref[...], wg_ref[...])
    proj = _dot(p_ref[...].astype(BF16), wp_ref[...])
    y = x_ref[...] + jax.nn.sigmoid(gate) * proj
    o_ref[...] = y
    ob_ref[...] = y.astype(BF16)


def _ple(x, xb, p, w_gate, w_proj, layer, *, name):
    m, d = x.shape
    pd = p.shape[-1]
    assert d * d * w_gate.dtype.itemsize <= RESIDENT_WEIGHT_BYTES
    tm = _tile(m, 512)
    row = lambda width: pl.BlockSpec((tm, width), lambda i: (i, 0))
    return pl.pallas_call(
        _ple_kernel,
        out_shape=(jax.ShapeDtypeStruct((m, d), F32), jax.ShapeDtypeStruct((m, d), BF16)),
        grid=(m // tm,),
        in_specs=[
            row(d), row(d),
            pl.BlockSpec((None, tm, pd), lambda i: (layer, i, 0)),
            _resident((None, d, d), lambda i: (layer, 0, 0)),
            _resident((None, pd, d), lambda i: (layer, 0, 0)),
        ],
        out_specs=(row(d), row(d)),
        compiler_params=_cparams("parallel"),
        name=name,
    )(x, xb, p, w_gate, w_proj)


LOG2E = 1.4426950408889634


def _cumsum_kernel(x_ref, tri_ref, f_ref, ft_ref, *, seq, heads):
    cl = SSD_CHUNK
    tri = tri_ref[...]
    carry = jnp.zeros((1, LANES), F32)
    for c in range(seq // cl):
        rs = slice(c * cl, (c + 1) * cl)
        fc = _sel_dot(tri, x_ref[0, rs, :], 3) + carry
        f2 = fc * LOG2E
        f_ref[0, rs, :] = f2
        ft_ref[0, :, rs] = f2.T[0:heads, :]
        carry = fc[cl - 1:cl, :]


def _forget_cumsum(logf_pad, *, heads, name):
    bsz, seq, _ = logf_pad.shape
    cl = SSD_CHUNK
    tri = (jnp.arange(cl)[:, None] >= jnp.arange(cl)[None, :]).astype(BF16)
    return pl.pallas_call(
        functools.partial(_cumsum_kernel, seq=seq, heads=heads),
        out_shape=(jax.ShapeDtypeStruct((bsz, seq, LANES), F32), jax.ShapeDtypeStruct((bsz, heads, seq), F32)),
        grid=(bsz,),
        in_specs=[pl.BlockSpec((1, seq, LANES), lambda b: (b, 0, 0)), pl.BlockSpec((cl, cl), lambda b: (0, 0))],
        out_specs=(pl.BlockSpec((1, seq, LANES), lambda b: (b, 0, 0)), pl.BlockSpec((1, heads, seq), lambda b: (b, 0, 0))),
        compiler_params=_cparams("parallel"),
        name=name,
    )(logf_pad, tri)


ATT_BLOCK = 512


def _flash_kernel(q_ref, k_ref, v_ref, f_ref, ft_ref, o_ref, m_ref, l_ref, acc_ref, fq_ref, *, heads):
    qi = pl.program_id(1)
    ki = pl.program_id(2)
    tq = q_ref.shape[1]
    tk = k_ref.shape[1]
    hd = ATT_HEAD_DIM
    @pl.when(ki == 0)
    def _():
        m_ref[...] = jnp.full_like(m_ref, NEG_INF)
        l_ref[...] = jnp.zeros_like(l_ref)
        acc_ref[...] = jnp.zeros_like(acc_ref)
        f_q = f_ref[0]
        for h in range(heads):
            fq_ref[h] = jnp.broadcast_to(f_q[:, h:h + 1], (tq, LANES))

    rep = tk // LANES

    def lanes_to_tk(v):
        return jnp.concatenate([v] * rep, axis=1)

    def step(diagonal):
        f_k = ft_ref[0]
        if diagonal:
            mask = lax.broadcasted_iota(jnp.int32, (tq, tk), 0) >= lax.broadcasted_iota(jnp.int32, (tq, tk), 1)
        for h in range(heads):
            hs = slice(h * hd, (h + 1) * hd)
            s = _dot_nt(q_ref[0, :, hs], k_ref[0, :, hs])
            x = s + (lanes_to_tk(fq_ref[h]) - f_k[h:h + 1, :])
            if diagonal:
                x = jnp.where(mask, x, NEG_INF)
            m_prev = m_ref[h]
            m_new = jnp.maximum(m_prev, jnp.max(x, axis=-1, keepdims=True))
            p = jnp.exp2(x - lanes_to_tk(m_new))
            alpha = jnp.exp2(m_prev - m_new)
            l_ref[h] = alpha * l_ref[h] + jnp.sum(p, axis=-1, keepdims=True)
            acc_ref[h] = alpha * acc_ref[h] + _dot(p.astype(BF16), v_ref[0, :, hs])
            m_ref[h] = m_new

    @pl.when(ki < qi)
    def _():
        step(False)

    @pl.when(ki == qi)
    def _():
        step(True)
        for h in range(heads):
            o_ref[0, :, h * hd:(h + 1) * hd] = (acc_ref[h] / l_ref[h]).astype(o_ref.dtype)


def _flash_prompt(qb, kb, vb, f, ft, *, name):
    bsz, seq, d = qb.shape
    heads = d // ATT_HEAD_DIM
    t = _tile(seq, ATT_BLOCK)
    nb = seq // t
    return pl.pallas_call(
        functools.partial(_flash_kernel, heads=heads),
        out_shape=jax.ShapeDtypeStruct((bsz, seq, d), BF16),
        grid=(bsz, nb, nb),
        in_specs=[
            pl.BlockSpec((1, t, d), lambda b, i, j: (b, i, 0)),
            pl.BlockSpec((1, t, d), lambda b, i, j: (b, jnp.minimum(i, j), 0)),
            pl.BlockSpec((1, t, d), lambda b, i, j: (b, jnp.minimum(i, j), 0)),
            pl.BlockSpec((1, t, LANES), lambda b, i, j: (b, i, 0)),
            pl.BlockSpec((1, heads, t), lambda b, i, j: (b, 0, jnp.minimum(i, j))),
        ],
        out_specs=pl.BlockSpec((1, t, d), lambda b, i, j: (b, i, 0)),
        scratch_shapes=[pltpu.VMEM((heads, t, LANES), F32), pltpu.VMEM((heads, t, LANES), F32),
                        pltpu.VMEM((heads, t, ATT_HEAD_DIM), F32), pltpu.VMEM((heads, t, LANES), F32)],
        compiler_params=_cparams("parallel", "parallel", "arbitrary"),
        name=name,
    )(qb, kb, vb, f, ft)


PAGES_PER_STEP = 8


def _decode_kernel(pt_ref, q_ref, kn_ref, vn_ref, fn_ref, *rest, heads, tnew, pps):
    del pt_ref
    k_refs = rest[0:pps]
    v_refs = rest[pps:2 * pps]
    lf_refs = rest[2 * pps:3 * pps]
    a_ref, a2_ref, sur_ref, o_ref = rest[3 * pps:3 * pps + 4]
    m_ref, l_ref, acc_ref, carry_ref, cnrep_ref, madd_ref, cnt_ref = rest[3 * pps + 4:]
    j = pl.program_id(1)
    hd = ATT_HEAD_DIM
    page = k_refs[0].shape[1]
    nq = tnew * heads
    nk = page * heads
    fr = nk // LANES

    def widen(v, w):
        return v[:, 0:w] if w <= LANES else jnp.concatenate([v] * (w // LANES), axis=1)

    def update(xs, vbs):
        m_prev = m_ref[...]
        m_new = m_prev
        for x in xs:
            m_new = jnp.maximum(m_new, jnp.max(x, axis=-1, keepdims=True))
        alpha = jnp.exp(m_prev - m_new)
        l_new = alpha * l_ref[...]
        acc = alpha * acc_ref[...]
        for x, vb in zip(xs, vbs):
            p = jnp.exp(x - widen(m_new, x.shape[1]))
            l_new = l_new + jnp.sum(p, axis=-1, keepdims=True)
            acc = acc + _dot(p.astype(BF16), vb)
        l_ref[...] = l_new
        acc_ref[...] = acc
        m_ref[...] = m_new

    @pl.when(j == 0)
    def _init():
        m_ref[...] = jnp.full_like(m_ref, NEG_INF)
        l_ref[...] = jnp.zeros_like(l_ref)
        acc_ref[...] = jnp.zeros_like(acc_ref)
        carry_ref[...] = jnp.zeros_like(carry_ref)
        cnt_ref[...] = jnp.zeros_like(cnt_ref)
        run = jnp.zeros((1, LANES), F32)
        for t in range(tnew):
            run = run + fn_ref[0, t:t + 1, :]
            cnt_ref[t:t + 1, :] = run
        cn = cnt_ref[...]
        cn_t = cn.T
        cncol = jnp.concatenate([cn_t[0:heads, t:t + 1] for t in range(tnew)], axis=0)
        cnrep_ref[...] = jnp.broadcast_to(cncol, (nq, LANES))
        r = lax.broadcasted_iota(jnp.int32, (nq, nk), 0)
        c = lax.broadcasted_iota(jnp.int32, (nq, nk), 1)
        madd_ref[...] = jnp.where(r % heads == c % heads, jnp.broadcast_to(cncol, (nq, nk)), NEG_INF)
        s = _dot_nt(q_ref[0], kn_ref[0].astype(BF16))
        brow = jnp.concatenate([cn[t:t + 1, 0:heads] for t in range(tnew)], axis=1)
        x = s + (cnrep_ref[:, 0:nq] - brow)
        rr = lax.broadcasted_iota(jnp.int32, (nq, nq), 0)
        cc = lax.broadcasted_iota(jnp.int32, (nq, nq), 1)
        ok = jnp.logical_and(rr % heads == cc % heads, cc // heads <= rr // heads)
        update([jnp.where(ok, x, NEG_INF)], [vn_ref[0].astype(BF16)])

    a_sel = a_ref[...]
    a2_sel = a2_ref[...]
    sur = sur_ref[...]
    q = q_ref[0]
    xs, vbs = [], []
    for i in range(pps):
        lf = lf_refs[i][0]
        rowtot = _dot_sel(lf, a2_sel, 3)
        suf = _dot_sel(lf, a_sel, 3) + _sel_dot(sur, rowtot, 3) + carry_ref[0:1, :]
        carry_ref[0:1, :] = carry_ref[0:1, :] + jnp.sum(rowtot, axis=0, keepdims=True)
        brow = jnp.concatenate([suf[rw:rw + 1, :] for rw in range(fr)], axis=1)
        k2 = k_refs[i][0].reshape(nk, hd).astype(BF16)
        xs.append(_dot_nt(q, k2) + (brow + madd_ref[...]))
        vbs.append(v_refs[i][0].reshape(nk, hd).astype(BF16))
    update(xs, vbs)

    @pl.when(j == pl.num_programs(1) - 1)
    def _fin():
        o_ref[0] = acc_ref[...] / l_ref[...]


def _decode_attention(qb, k_new, v_new, f_new, cache_k, cache_v, cache_logf, page_table, *, name):
    bsz, tnew, d = qb.shape
    n_pool, page, heads, hd = cache_k.shape
    n_pages = page_table.shape[1]
    pps = PAGES_PER_STEP
    nq = tnew * heads
    nk = page * heads
    fr = nk // LANES
    assert n_pages % pps == 0 and tnew <= SUBLANES and hd == ATT_HEAD_DIM and heads * hd == d
    assert LANES % heads == 0 and nk % LANES == 0
    lane = jnp.arange(LANES)
    same_head = (lane[:, None] % heads) == (lane[None, :] % heads)
    a2_sel = same_head.astype(BF16)
    a_sel = jnp.logical_and(same_head, lane[:, None] // heads > lane[None, :] // heads).astype(BF16)
    sur = (jnp.arange(fr)[None, :] > jnp.arange(fr)[:, None]).astype(BF16)
    lf_flat = cache_logf.reshape(n_pool, fr, LANES)

    def slot(b, j, pt, i):
        return pt[b, n_pages - 1 - (j * pps + i)]

    def kv_spec(i):
        return pl.BlockSpec((1, page, heads, hd), lambda b, j, pt: (slot(b, j, pt, i), 0, 0, 0))

    def lf_spec(i):
        return pl.BlockSpec((1, fr, LANES), lambda b, j, pt: (slot(b, j, pt, i), 0, 0))

    per_b = lambda shape: pl.BlockSpec(shape, lambda b, j, pt: (b, 0, 0))
    const = lambda shape: pl.BlockSpec(shape, lambda b, j, pt: (0, 0))
    in_specs = [per_b((1, nq, hd)), per_b((1, nq, hd)), per_b((1, nq, hd)), per_b((1, tnew, LANES))]
    in_specs += [kv_spec(i) for i in range(pps)]
    in_specs += [kv_spec(i) for i in range(pps)]
    in_specs += [lf_spec(i) for i in range(pps)]
    in_specs += [const((LANES, LANES)), const((LANES, LANES)), const((fr, fr))]
    grid_spec = pltpu.PrefetchScalarGridSpec(
        num_scalar_prefetch=1,
        grid=(bsz, n_pages // pps),
        in_specs=in_specs,
        out_specs=per_b((1, nq, hd)),
        scratch_shapes=[
            pltpu.VMEM((nq, LANES), F32),
            pltpu.VMEM((nq, LANES), F32),
            pltpu.VMEM((nq, hd), F32),
            pltpu.VMEM((SUBLANES, LANES), F32),
            pltpu.VMEM((nq, LANES), F32),
            pltpu.VMEM((nq, nk), F32),
            pltpu.VMEM((SUBLANES, LANES), F32),
        ],
    )
    out = pl.pallas_call(
        functools.partial(_decode_kernel, heads=heads, tnew=tnew, pps=pps),
        out_shape=jax.ShapeDtypeStruct((bsz, nq, hd), F32),
        grid_spec=grid_spec,
        compiler_params=_cparams("parallel", "arbitrary"),
        name=name,
    )(page_table, qb.reshape(bsz, nq, hd), k_new, v_new, f_new,
      *([cache_k] * pps), *([cache_v] * pps), *([lf_flat] * pps), a_sel, a2_sel, sur)
    return out.reshape(bsz, tnew, d)


def _run(x, p, conv0, ssm0, w, tag, attend, q_scale):
    bsz, seq, d = x.shape
    m = bsz * seq
    x2 = x.reshape(m, d)
    d_inner = w["d_inner"]
    heads = d // ATT_HEAD_DIM

    p = p.reshape(p.shape[0], m, p.shape[-1])
    ffn = lambda x2, xb, layer: _ffn(x2, xb, w["wg"], w["wu"], w["wd"], w["ln2_g"], w["ln2_b"], layer,
                                     name=f"ffn{layer}_{tag}")
    ple = lambda x2, xb, layer: _ple(x2, xb, p, w["ple_g"], w["ple_p"], layer, name=f"ple{layer}_{tag}")

    zx, xb = _proj(x2, w["w_in"], cols=(0, w["zx_dim"]), mode="cast_x", name=f"in_proj_{tag}")
    dt_raw = _proj(xb, w["w_dt"], name=f"dt_proj_{tag}")
    act, new_conv = _conv_silu(zx.reshape(bsz, seq, -1), conv0, w["conv_w"], w["conv_b"], d_inner=d_inner,
                               name=f"conv_{tag}")
    g, new_ssm = _ssd(zx.reshape(bsz, seq, -1), act, dt_raw.reshape(bsz, seq, LANES), ssm0, w["dt_bias"], w["a_log"],
                      w["d_skip"], w["norm_w"], d_inner=d_inner, name=f"ssd_{tag}")
    x2, xb = _mm_ln(g.reshape(m, d_inner), w["w_out"], x2, w["ln1_g"], w["ln1_b"], 0, name=f"out_proj_ln_{tag}")
    x2, xb = ffn(x2, xb, 0)
    x2, xb = ple(x2, xb, 0)

    k, kb = _proj(xb, w["w_kv"], cols=(0, d), mode="dual", name=f"k_proj_{tag}")
    v, vb = _proj(xb, w["w_kv"], cols=(d, d), mode="dual", name=f"v_proj_{tag}")
    logf_pad = _proj(xb, w["w_f"], mode="logsig", bias=w["b_f"], name=f"f_proj_{tag}")

    qb = _proj(xb, w["w_q"], mode="scale", scale=q_scale, out_dtype=BF16, name=f"q_proj_{tag}")
    rows_heads = (bsz, seq * heads, ATT_HEAD_DIM)
    o = attend(qb.reshape(bsz, seq, d), k.reshape(rows_heads), v.reshape(rows_heads), kb.reshape(bsz, seq, d),
               vb.reshape(bsz, seq, d), logf_pad.reshape(bsz, seq, LANES))
    x2, xb = _mm_ln(o.reshape(m, d).astype(BF16), w["w_o"], x2, w["ln1_g"], w["ln1_b"], 1, name=f"o_proj_ln_{tag}")
    x2, xb = ffn(x2, xb, 1)
    y, _ = ple(x2, xb, 1)

    return (y.reshape(bsz, seq, d), new_conv[None], new_ssm[None],
            k.reshape(bsz, seq, heads, ATT_HEAD_DIM), v.reshape(bsz, seq, heads, ATT_HEAD_DIM),
            logf_pad.reshape(bsz, seq, LANES)[..., :heads])


def kernel(x_prompt, x_sample, p_prompt, p_sample, state_conv, state_ssm, cache_k, cache_v, cache_logf, page_table, a_w_in, a_conv_w, a_conv_b, a_dt_bias, a_log, a_d, a_norm_w, a_w_out, kv_w, kv_b_f, b_w_q, b_w_o, ln1_g, ln1_b, ffn_w_gate, ffn_w_up, ffn_w_down, ln2_g, ln2_b, ple_w_gate, ple_w_proj):
    assert a_w_in.shape[0] == 1 and b_w_q.shape[0] == 1, "one SSD layer followed by one attention layer"
    d = x_prompt.shape[-1]
    heads = d // ATT_HEAD_DIM
    d_att = heads * ATT_HEAD_DIM
    conv_dim = a_conv_w.shape[-1]
    d_inner = conv_dim - 2 * SSM_GROUPS * SSM_STATE
    ssm_heads = d_inner // SSM_HEAD_DIM
    zx_dim = d_inner + conv_dim

    def pad_cols(wm):
        return jnp.pad(wm, ((0, 0), (0, LANES - wm.shape[1])))

    bf = lambda t: t.astype(BF16)
    per_layer_vec = lambda t: t.astype(F32).reshape(t.shape[0], 1, t.shape[1])
    w = dict(
        d_inner=d_inner, zx_dim=zx_dim,
        w_in=bf(a_w_in), w_dt=bf(pad_cols(a_w_in[0][:, zx_dim:])),
        conv_w=a_conv_w[0], conv_b=a_conv_b[0], dt_bias=a_dt_bias[0], a_log=a_log[0], d_skip=a_d[0],
        norm_w=a_norm_w[0], w_out=bf(a_w_out),
        w_kv=bf(kv_w), w_f=bf(pad_cols(kv_w[:, 2 * d_att:])),
        b_f=jnp.pad(kv_b_f.astype(F32), (0, LANES - heads)).reshape(1, LANES),
        w_q=bf(b_w_q), w_o=bf(b_w_o),
        ln1_g=per_layer_vec(ln1_g), ln1_b=per_layer_vec(ln1_b), ln2_g=per_layer_vec(ln2_g), ln2_b=per_layer_vec(ln2_b),
        wg=bf(ffn_w_gate), wu=bf(ffn_w_up), wd=bf(ffn_w_down), ple_g=bf(ple_w_gate), ple_p=bf(ple_w_proj),
    )
    assert ssm_heads * SSM_HEAD_DIM == d_inner

    def attend_prompt(qb, k, v, kb, vb, logf_pad):
        f, ft = _forget_cumsum(logf_pad, heads=heads, name="forget_cumsum")
        return _flash_prompt(qb, kb, vb, f, ft, name="flash_prompt")

    def attend_sample(qb, k, v, kb, vb, logf_pad):
        return _decode_attention(qb, k, v, logf_pad, cache_k, cache_v, cache_logf, page_table, name="decode_attn")

    bp = x_prompt.shape[0]
    conv0_p = jnp.zeros((bp, CONV_WIDTH - 1, conv_dim), F32)
    att_scale = ATT_HEAD_DIM ** -0.5
    outs_p = _run(x_prompt, p_prompt, conv0_p, None, w, "p", attend_prompt, att_scale * LOG2E)
    outs_s = _run(x_sample, p_sample, state_conv[0], state_ssm[0], w, "s", attend_sample, att_scale)
    y_p, conv_p, ssm_p, k_p, v_p, f_p = outs_p
    y_s, conv_s, ssm_s, k_s, v_s, f_s = outs_s
    return (y_p, y_s, conv_p, ssm_p, k_p, v_p, f_p, conv_s, ssm_s, k_s, v_s, f_s)
```
